```python
import math
import jax, jax.numpy as jnp
from jax import lax
import numpy as np

D_MODEL = 1024
BATCH = 8
SEQ = 4096
DEPTH = 2

N_MIXERS = 2
N_CONV_LAYERS = (DEPTH + 1) // 2
N_ATTN_LAYERS = DEPTH // 2
CONV_CH = D_MODEL
CONV_WIDTH = 31
DA_HEADS = D_MODEL // 128
DA_HEAD_DIM = 64
DA_V_DIM = 2 * DA_HEAD_DIM
ROPE_DIM = DA_HEAD_DIM // 4
ROPE_THETA = 500000.0
Q_BLOCK = 128
N_GROUPS = 4
EXPERTS_PER_GROUP = 8
N_EXPERTS = N_GROUPS * EXPERTS_PER_GROUP
TOP_K_IN_GROUP = 2
D_EXPERT = D_MODEL // 4
PLE_DIM = 256
EPS = 1e-6

kernel_name = "hybrid_conformer_diffattn_hmoe"


def rms_norm(x, g):
    xf = x.astype(jnp.float32)
    y = xf * lax.rsqrt(jnp.mean(xf * xf, axis=-1, keepdims=True) + EPS)
    return (y * g.astype(jnp.float32)).astype(x.dtype)


def layer_norm(x, g, b):
    xf = x.astype(jnp.float32)
    mu = jnp.mean(xf, axis=-1, keepdims=True)
    var = jnp.mean(jnp.square(xf - mu), axis=-1, keepdims=True)
    y = (xf - mu) * lax.rsqrt(var + EPS)
    return (y * g.astype(jnp.float32) + b.astype(jnp.float32)).astype(x.dtype)


def conformer_conv(h, w_pw1, b_pw1, w_dw, b_dw, ln_g, ln_b, w_pw2, b_pw2):
    u = h @ w_pw1 + b_pw1
    a, g = jnp.split(u, 2, axis=-1)
    v = a * jax.nn.sigmoid(g)
    v = lax.conv_general_dilated(
        v, w_dw[:, None, :], window_strides=(1,),
        padding=[(CONV_WIDTH - 1, 0)],
        dimension_numbers=("NWC", "WIO", "NWC"),
        feature_group_count=CONV_CH) + b_dw
    v = jax.nn.silu(layer_norm(v, ln_g, ln_b))
    return v @ w_pw2 + b_pw2


def rope_tables(positions):
    inv_freq = ROPE_THETA ** (-jnp.arange(0, ROPE_DIM, 2, dtype=jnp.float32) / ROPE_DIM)
    ang = positions.astype(jnp.float32)[..., None] * inv_freq
    return jnp.cos(ang)[:, :, None, None, :], jnp.sin(ang)[:, :, None, None, :]


def partial_rope(x, cos, sin):
    half = ROPE_DIM // 2
    x1 = x[..., :half].astype(jnp.float32)
    x2 = x[..., half:ROPE_DIM].astype(jnp.float32)
    rot = jnp.concatenate([x1 * cos - x2 * sin, x2 * cos + x1 * sin], axis=-1)
    return jnp.concatenate([rot.astype(x.dtype), x[..., ROPE_DIM:]], axis=-1)


def diff_attention(h, cos, sin, w_qkv, lam_params, subln_g, w_o, lam_init):
    B, S, _ = h.shape
    qkv = h @ w_qkv
    nq = DA_HEADS * 2 * DA_HEAD_DIM
    q = qkv[..., :nq].reshape(B, S, DA_HEADS, 2, DA_HEAD_DIM)
    k = qkv[..., nq:2 * nq].reshape(B, S, DA_HEADS, 2, DA_HEAD_DIM)
    v = qkv[..., 2 * nq:].reshape(B, S, DA_HEADS, DA_V_DIM)
    q = partial_rope(q, cos, sin)
    k = partial_rope(k, cos, sin)
    lp = lam_params.astype(jnp.float32)
    lam = jnp.exp(jnp.sum(lp[0] * lp[1])) - jnp.exp(jnp.sum(lp[2] * lp[3])) + lam_init
    scale = DA_HEAD_DIM ** -0.5
    neg = jnp.finfo(jnp.float32).min
    outs = []
    for blk in range(S // Q_BLOCK):
        q0 = blk * Q_BLOCK
        kend = q0 + Q_BLOCK
        qb = q[:, q0:kend]
        kb = k[:, :kend]
        vb = v[:, :kend]
        s = jnp.einsum("bqhcd,bkhcd->bhcqk", qb, kb).astype(jnp.float32) * scale
        mask = (q0 + jnp.arange(Q_BLOCK))[:, None] >= jnp.arange(kend)[None, :]
        a = jax.nn.softmax(jnp.where(mask, s, neg), axis=-1)
        wgt = (a[:, :, 0] - lam * a[:, :, 1]).astype(vb.dtype)
        outs.append(jnp.einsum("bhqk,bkhd->bqhd", wgt, vb))
    o = jnp.concatenate(outs, axis=1)
    o = rms_norm(o, subln_g) * (1.0 - lam_init)
    return o.reshape(B, S, DA_HEADS * DA_V_DIM) @ w_o


def hier_moe(h, w_rg, b_rg, w_re, b_re, w_gate, w_up, w_down):
    gl = (h @ w_rg).astype(jnp.float32) + b_rg
    g_idx = jnp.argmax(gl, axis=-1)
    g_oh = jax.nn.one_hot(g_idx, N_GROUPS, dtype=jnp.float32)
    g_w = jnp.sum(jax.nn.softmax(gl, axis=-1) * g_oh, axis=-1, keepdims=True)
    el = jnp.einsum("bsd,gde->bsge", h, w_re).astype(jnp.float32) + b_re
    el_sel = jnp.einsum("bsge,bsg->bse", el, g_oh)
    top_v, top_i = lax.top_k(el_sel, TOP_K_IN_GROUP)
    top_w = jax.nn.softmax(top_v, axis=-1) * g_w
    e_oh = jax.nn.one_hot(top_i, EXPERTS_PER_GROUP, dtype=jnp.float32)
    within = jnp.sum(e_oh * top_w[..., None], axis=-2)
    combine = (g_oh[..., :, None] * within[..., None, :]).astype(h.dtype)
    out = jnp.zeros_like(h)
    for g in range(N_GROUPS):
        hg = jnp.einsum("bsd,edf->bsef", h, w_gate[g])
        ug = jnp.einsum("bsd,edf->bsef", h, w_up[g])
        act = jax.nn.silu(hg) * ug * combine[:, :, g, :, None]
        out = out + jnp.einsum("bsef,efd->bsd", act, w_down[g])
    return out


def setup_inputs(seed: int = 0) -> dict:
    key = jax.random.key(seed)
    ks = iter(jax.random.split(key, 40))
    nrm = lambda shape, scale: jax.random.normal(next(ks), shape, jnp.float32) * scale
    gain = lambda shape: 1.0 + nrm(shape, 0.02)
    D, C = D_MODEL, CONV_CH
    nc, na = N_CONV_LAYERS, N_ATTN_LAYERS
    nq = DA_HEADS * 2 * DA_HEAD_DIM
    G, E, F = N_GROUPS, EXPERTS_PER_GROUP, D_EXPERT
    x = jax.random.normal(next(ks), (BATCH, SEQ, D), jnp.float32)
    p = jax.random.normal(next(ks), (DEPTH, BATCH, SEQ, PLE_DIM), jnp.float32)
    offs = jax.random.randint(next(ks), (BATCH, 1), 0, 1024, dtype=jnp.int32)
    positions = offs + jnp.arange(SEQ, dtype=jnp.int32)[None, :]
    return {
        "x": x, "p": p, "positions": positions,
        "norm_mix": gain((DEPTH, D)), "norm_ffn": gain((DEPTH, D)),
        "conv_w_pw1": nrm((nc, D, 2 * C), D ** -0.5), "conv_b_pw1": nrm((nc, 2 * C), 0.01),
        "conv_w_dw": nrm((nc, CONV_WIDTH, C), CONV_WIDTH ** -0.5), "conv_b_dw": nrm((nc, C), 0.01),
        "conv_ln_g": gain((nc, C)), "conv_ln_b": nrm((nc, C), 0.01),
        "conv_w_pw2": nrm((nc, C, D), C ** -0.5), "conv_b_pw2": nrm((nc, D), 0.01),
        "da_w_qkv": nrm((na, D, 2 * nq + DA_HEADS * DA_V_DIM), D ** -0.5),
        "da_lambda": nrm((na, 4, DA_HEAD_DIM), 0.1),
        "da_subln": gain((na, DA_V_DIM)),
        "da_w_o": nrm((na, DA_HEADS * DA_V_DIM, D), (DA_HEADS * DA_V_DIM) ** -0.5),
        "moe_w_rg": nrm((DEPTH, D, G), D ** -0.5), "moe_b_rg": nrm((DEPTH, G), 0.01),
        "moe_w_re": nrm((DEPTH, G, D, E), D ** -0.5), "moe_b_re": nrm((DEPTH, G, E), 0.01),
        "moe_w_gate": nrm((DEPTH, G, E, D, F), D ** -0.5),
        "moe_w_up": nrm((DEPTH, G, E, D, F), D ** -0.5),
        "moe_w_down": nrm((DEPTH, G, E, F, D), F ** -0.5),
        "ple_norm": gain((DEPTH, D)),
        "ple_w_gate": nrm((DEPTH, D, D), D ** -0.5),
        "ple_w_proj": nrm((DEPTH, PLE_DIM, D), PLE_DIM ** -0.5),
        "final_norm": gain((D,)),
    }


def reference(x, p, positions, norm_mix, norm_ffn,
              conv_w_pw1, conv_b_pw1, conv_w_dw, conv_b_dw, conv_ln_g, conv_ln_b,
              conv_w_pw2, conv_b_pw2,
              da_w_qkv, da_lambda, da_subln, da_w_o,
              moe_w_rg, moe_b_rg, moe_w_re, moe_b_re, moe_w_gate, moe_w_up, moe_w_down,
              ple_norm, ple_w_gate, ple_w_proj, final_norm):
    cos, sin = rope_tables(positions)
    for i in range(DEPTH):
        j = i // N_MIXERS
        h = rms_norm(x, norm_mix[i])
        if i % N_MIXERS == 0:
            y = conformer_conv(h, conv_w_pw1[j], conv_b_pw1[j], conv_w_dw[j], conv_b_dw[j],
                               conv_ln_g[j], conv_ln_b[j], conv_w_pw2[j], conv_b_pw2[j])
        else:
            lam_init = 0.8 - 0.6 * math.exp(-0.3 * i)
            y = diff_attention(h, cos, sin, da_w_qkv[j], da_lambda[j], da_subln[j],
                               da_w_o[j], lam_init)
        x = x + y
        h = rms_norm(x, norm_ffn[i])
        x = x + hier_moe(h, moe_w_rg[i], moe_b_rg[i], moe_w_re[i], moe_b_re[i],
                         moe_w_gate[i], moe_w_up[i], moe_w_down[i])
        gate = jax.nn.sigmoid(rms_norm(x, ple_norm[i]) @ ple_w_gate[i])
        x = x + gate * (p[i] @ ple_w_proj[i])
    return rms_norm(x, final_norm)
```

```python
import functools
import math

import numpy as np
import jax
import jax.numpy as jnp
from jax import lax
from jax.experimental import pallas as pl
from jax.experimental.pallas import tpu as pltpu

F32 = jnp.float32
BF16 = jnp.bfloat16
I32 = jnp.int32

EPS = 1e-6
CONV_WIDTH = 31
CONV_HALO = 32
HEAD_DIM = 64
V_DIM = 128
ROPE_DIM = 16
ROPE_THETA = 500000.0
N_GROUPS = 4
N_EXP = 8
N_PAIRS = N_EXP * (N_EXP - 1) // 2
N_CLASSES = N_GROUPS * N_PAIRS
LANES = 128
SUBLANES = 8
VMEM_LIMIT = 56 * 1024 * 1024

TILE_CONV = 512
TILE_ROUTE = 512
TILE_DISP = 512
TILE_EXP = 256
TILE_PLE = 512
TILE_QKV = 512
TILE_Q = 512
TILE_K = 256
CONV_ROWS = 64
CONV_LANES = 128


def _pair_tables():
    remaining = list(range(N_EXP))
    order = []
    center, center_slot = 0, 0
    while len(remaining) > 1:
        others = [v for v in remaining if v != center]
        for v in others:
            order.append((center, v) if center_slot == 0 else (v, center))
        remaining.remove(center)
        center, center_slot = others[-1], 1 - center_slot
    assert len(order) == N_PAIRS and len({frozenset(p) for p in order}) == N_PAIRS
    cls = np.zeros((N_EXP * N_EXP,), np.float32)
    first_is_a = np.zeros((N_EXP * N_EXP,), np.float32)
    for idx, (a, b) in enumerate(order):
        cls[a * N_EXP + b] = idx
        cls[b * N_EXP + a] = idx
        first_is_a[a * N_EXP + b] = 1.0
    tbl = np.zeros((8, LANES), np.float32)
    tbl[0, : N_EXP * N_EXP] = cls
    tbl[1, : N_EXP * N_EXP] = first_is_a
    exp_a = np.array([g * N_EXP + a for g in range(N_GROUPS) for (a, _) in order], np.int32)
    exp_b = np.array([g * N_EXP + b for g in range(N_GROUPS) for (_, b) in order], np.int32)
    return tbl, exp_a, exp_b


_PAIR_TBL, _CLASS_EXP_A, _CLASS_EXP_B = _pair_tables()


def _rms(x, g):
    return x * lax.rsqrt(jnp.mean(x * x, axis=-1, keepdims=True) + EPS) * g


def _params(*sem):
    return pltpu.CompilerParams(dimension_semantics=sem, vmem_limit_bytes=VMEM_LIMIT)


def _conv_kernel(x_ref, g_ref, w1_ref, b1_ref, wdw_ref, bdw_ref, lng_ref, lnb_ref, w2_ref, b2_ref,
                 o_ref, vbuf, cbuf):
    t, c = cbuf.shape
    x = x_ref[0]
    h = _rms(x, g_ref[...])
    u = jnp.dot(h.astype(BF16), w1_ref[...], preferred_element_type=F32) + b1_ref[...]
    v = u[:, :c] * jax.nn.sigmoid(u[:, c:])

    @pl.when(pl.program_id(1) == 0)
    def _():
        vbuf[0:CONV_HALO, :] = jnp.zeros((CONV_HALO, c), F32)

    @pl.when(pl.program_id(1) > 0)
    def _():
        vbuf[0:CONV_HALO, :] = vbuf[t:t + CONV_HALO, :]

    vbuf[CONV_HALO:CONV_HALO + t, :] = v

    first = CONV_HALO - (CONV_WIDTH - 1)
    for c0 in range(0, c, CONV_LANES):
        for r0 in range(0, t, CONV_ROWS):
            acc = jnp.zeros((CONV_ROWS, CONV_LANES), F32)
            for k in range(CONV_WIDTH):
                acc = acc + wdw_ref[k:k + 1, c0:c0 + CONV_LANES] * vbuf[r0 + first + k:r0 + first + k + CONV_ROWS,
                                                                        c0:c0 + CONV_LANES]
            cbuf[r0:r0 + CONV_ROWS, c0:c0 + CONV_LANES] = acc + bdw_ref[:, c0:c0 + CONV_LANES]

    y = cbuf[...]
    mu = jnp.mean(y, axis=-1, keepdims=True)
    d = y - mu
    var = jnp.mean(d * d, axis=-1, keepdims=True)
    y = d * lax.rsqrt(var + EPS) * lng_ref[...] + lnb_ref[...]
    y = y * jax.nn.sigmoid(y)
    o_ref[0] = x + jnp.dot(y.astype(BF16), w2_ref[...], preferred_element_type=F32) + b2_ref[...]


def _conv_mixer(x, g, w1, b1, wdw, bdw, lng, lnb, w2, b2):
    b, s, d = x.shape
    c = w2.shape[0]
    t = min(TILE_CONV, s)
    row = lambda a: a.reshape(1, -1)
    full = lambda a: pl.BlockSpec(a.shape, lambda i, j: (0,) * a.ndim)
    args = (row(g), w1.astype(BF16), row(b1), wdw, row(bdw), row(lng), row(lnb), w2.astype(BF16), row(b2))
    return pl.pallas_call(
        _conv_kernel,
        out_shape=jax.ShapeDtypeStruct(x.shape, F32),
        grid=(b, s // t),
        in_specs=[pl.BlockSpec((1, t, d), lambda i, j: (i, j, 0))] + [full(a) for a in args],
        out_specs=pl.BlockSpec((1, t, d), lambda i, j: (i, j, 0)),
        scratch_shapes=[pltpu.VMEM((CONV_HALO + t, c), F32), pltpu.VMEM((t, c), F32)],
        compiler_params=_params("arbitrary", "arbitrary"),
        name="conv_mixer",
    )(x, *args)


def _route_body(x, g_ref, wr_ref, br_ref, tri_ref, tbl_ref, hext_ref, meta_ref, cnt_ref, cnt_scr):
    t, d = x.shape

    @pl.when(pl.program_id(0) == 0)
    def _():
        cnt_scr[...] = jnp.zeros(cnt_scr.shape, F32)

    h = _rms(x, g_ref[...])
    lg = jnp.dot(h.astype(BF16), wr_ref[...], preferred_element_type=F32) + br_ref[...]
    lane = lax.broadcasted_iota(I32, (t, LANES), 1)
    ninf = -jnp.inf

    gmask = lane < N_GROUPS
    gl = jnp.where(gmask, lg, ninf)
    gmax = jnp.max(gl, axis=1, keepdims=True)
    gidx = jnp.min(jnp.where(gl == gmax, lane, LANES), axis=1, keepdims=True)
    gsum = jnp.sum(jnp.where(gmask, jnp.exp(lg - gmax), 0.0), axis=1, keepdims=True)
    gw = 1.0 / gsum

    lo = N_GROUPS + gidx * N_EXP
    el = jnp.where(lane >= lo, jnp.where(lane < lo + N_EXP, lg, ninf), ninf)
    m1 = jnp.max(el, axis=1, keepdims=True)
    i1 = jnp.min(jnp.where(el == m1, lane, LANES), axis=1, keepdims=True)
    el2 = jnp.where(lane == i1, ninf, el)
    m2 = jnp.max(el2, axis=1, keepdims=True)
    i2 = jnp.min(jnp.where(el2 == m2, lane, LANES), axis=1, keepdims=True)
    dlt = jnp.exp(m2 - m1)
    p1 = 1.0 / (1.0 + dlt)
    w1 = p1 * gw
    w2 = dlt * p1 * gw

    q = (i1 - lo) * N_EXP + (i2 - lo)
    qoh = lane == q
    cl = jnp.sum(jnp.where(qoh, tbl_ref[0:1, :], 0.0), axis=1, keepdims=True)
    fa = jnp.sum(jnp.where(qoh, tbl_ref[1:2, :], 0.0), axis=1, keepdims=True)
    cls = gidx * N_PAIRS + cl.astype(I32)
    first_a = fa > 0.5
    wa = jnp.where(first_a, w1, w2)
    wb = jnp.where(first_a, w2, w1)

    oh = lane == cls
    cum = jnp.dot(tri_ref[...], jnp.where(oh, 1.0, 0.0).astype(BF16), preferred_element_type=F32)
    cnt = cnt_scr[...]
    rank = jnp.sum(jnp.where(oh, cum + cnt, 0.0), axis=1, keepdims=True) - 1.0
    cnt = cnt + cum[t - 1:t, :]
    cnt_scr[...] = cnt
    cnt_ref[...] = cnt

    hext_ref[:, :d] = h
    hext_ref[:, d:] = jnp.where(lane == 0, wa, jnp.where(lane == 1, wb, 0.0))
    lane8 = lax.broadcasted_iota(I32, (t, 8), 1)
    meta_ref[...] = jnp.where(lane8 == 0, cls, jnp.where(lane8 == 1, rank.astype(I32), 0))


def _route0_kernel(x_ref, g_ref, wr_ref, br_ref, tri_ref, tbl_ref, hext_ref, meta_ref, cnt_ref, cnt_scr):
    _route_body(x_ref[...], g_ref, wr_ref, br_ref, tri_ref, tbl_ref, hext_ref, meta_ref, cnt_ref, cnt_scr)


def _route1_kernel(x_ref, o_ref, wo_ref, g_ref, wr_ref, br_ref, tri_ref, tbl_ref,
                   xo_ref, hext_ref, meta_ref, cnt_ref, cnt_scr):
    x = x_ref[...] + jnp.dot(o_ref[...], wo_ref[...], preferred_element_type=F32)
    xo_ref[...] = x
    _route_body(x, g_ref, wr_ref, br_ref, tri_ref, tbl_ref, hext_ref, meta_ref, cnt_ref, cnt_scr)


def _route(x, g, w_rg, b_rg, w_re, b_re, attn_o=None, w_o=None):
    n, d = x.shape
    t = min(TILE_ROUTE, n)
    wr = jnp.zeros((d, LANES), F32)
    wr = wr.at[:, :N_GROUPS].set(w_rg)
    wr = wr.at[:, N_GROUPS:N_GROUPS + N_GROUPS * N_EXP].set(jnp.transpose(w_re, (1, 0, 2)).reshape(d, -1))
    br = jnp.zeros((1, LANES), F32)
    br = br.at[0, :N_GROUPS].set(b_rg)
    br = br.at[0, N_GROUPS:N_GROUPS + N_GROUPS * N_EXP].set(b_re.reshape(-1))
    tri = jnp.asarray(np.tril(np.ones((t, t), np.float32)), BF16)
    tbl = jnp.asarray(_PAIR_TBL)
    full = lambda a: pl.BlockSpec(a.shape, lambda i: (0,) * a.ndim)
    rows = lambda w: pl.BlockSpec((t, w), lambda i: (i, 0))
    common = (g.reshape(1, -1), wr.astype(BF16), br, tri, tbl)
    out_shape = [jax.ShapeDtypeStruct((n, d + LANES), F32), jax.ShapeDtypeStruct((n, 8), I32),
                 jax.ShapeDtypeStruct((1, LANES), F32)]
    out_specs = [rows(d + LANES), rows(8), pl.BlockSpec((1, LANES), lambda i: (0, 0))]
    if attn_o is None:
        hext, meta, cnt = pl.pallas_call(
            _route0_kernel, out_shape=out_shape, grid=(n // t,),
            in_specs=[rows(d)] + [full(a) for a in common], out_specs=out_specs,
            scratch_shapes=[pltpu.VMEM((1, LANES), F32)],
            compiler_params=_params("arbitrary"), name="moe_route",
        )(x, *common)
        return x, hext, meta, cnt
    wo = w_o.astype(BF16)
    xo, hext, meta, cnt = pl.pallas_call(
        _route1_kernel, out_shape=[jax.ShapeDtypeStruct((n, d), F32)] + out_shape, grid=(n // t,),
        in_specs=[rows(d), rows(attn_o.shape[1]), full(wo)] + [full(a) for a in common],
        out_specs=[rows(d)] + out_specs,
        scratch_shapes=[pltpu.VMEM((1, LANES), F32)],
        compiler_params=_params("arbitrary"), name="attn_out_moe_route",
    )(x, attn_o, wo, *common)
    return xo, hext, meta, cnt


def _plan(meta, cnt, n):
    tm = TILE_EXP
    nt_max = n // tm + N_CLASSES
    cnt = cnt[0, :N_CLASSES].astype(I32)
    nt = (cnt + tm - 1) // tm
    tend = jnp.cumsum(nt)
    offs = (tend - nt) * tm
    total = tend[-1]
    dest = jnp.take(offs, meta[:, 0]) + meta[:, 1]
    tile = jnp.arange(nt_max, dtype=I32)
    rowblk = jnp.minimum(tile, total - 1)
    tcls = jnp.searchsorted(tend, rowblk, side="right").astype(I32)
    active = (tile < total).astype(I32)
    exp_a = jnp.take(jnp.asarray(_CLASS_EXP_A), tcls)
    exp_b = jnp.take(jnp.asarray(_CLASS_EXP_B), tcls)
    pad = jnp.zeros((LANES,), I32)
    padstart = pad.at[:N_CLASSES].set(offs + cnt)
    padstart = padstart.at[N_CLASSES].set(total * 2)
    padlen = pad.at[:N_CLASSES].set(nt * tm - cnt)
    return dest, rowblk, active, exp_a, exp_b, padstart, padlen, nt_max


_PAD_BITS = tuple(1 << b for b in reversed(range(3, int(math.log2(TILE_EXP)))))


def _dispatch_kernel(dest_ref, padstart_ref, padlen_ref, hext_ref, hs_ref, zbuf, sem):
    t = hext_ref.shape[0]
    i = pl.program_id(0)

    def row_copy(r, d):
        return pltpu.make_async_copy(hext_ref.at[pl.ds(r, 1)], hs_ref.at[pl.ds(d, 1)], sem.at[0])

    def issue(r, carry):
        row_copy(r, dest_ref[i * t + r]).start()
        return carry

    def drain(r, carry):
        row_copy(0, 0).wait()
        return carry

    lax.fori_loop(0, t, issue, 0)
    lax.fori_loop(0, t, drain, 0)

    @pl.when(i == pl.num_programs(0) - 1)
    def _():
        zbuf[...] = jnp.zeros(zbuf.shape, F32)

        def pad_copy(start, size):
            return pltpu.make_async_copy(zbuf.at[pl.ds(0, size)], hs_ref.at[pl.ds(start, size)], sem.at[1])

        def pieces(c, fn):
            start = padstart_ref[c]
            head = jnp.minimum((-start) & (SUBLANES - 1), padlen_ref[c])
            for j in range(SUBLANES - 1):
                @pl.when(j < head)
                def _(j=j):
                    fn(pad_copy(start + j, 1))

            start = start + head
            left = padlen_ref[c] - head
            for bit in _PAD_BITS:
                has = (left & bit) != 0

                @pl.when(has)
                def _(start=start, bit=bit):
                    fn(pad_copy(pl.multiple_of(start, SUBLANES), bit))

                start = start + jnp.where(has, bit, 0)

        def issue_pad(c, carry):
            pieces(c, lambda cp: cp.start())
            return carry

        def drain_pad(c, carry):
            pieces(c, lambda cp: cp.wait())
            return carry

        lax.fori_loop(0, N_CLASSES, issue_pad, 0)
        lax.fori_loop(0, N_CLASSES, drain_pad, 0)

        half = zbuf.shape[0]
        first_unused = padstart_ref[N_CLASSES]
        n_halves = hs_ref.shape[0] // half

        def tail_copy(j):
            return pad_copy(pl.multiple_of(j * half, half), half)

        def issue_tail(j, carry):
            tail_copy(j).start()
            return carry

        def drain_tail(j, carry):
            tail_copy(j).wait()
            return carry

        lax.fori_loop(first_unused, n_halves, issue_tail, 0)
        lax.fori_loop(first_unused, n_halves, drain_tail, 0)


def _dispatch(hext, dest, padstart, padlen, nt_max):
    n, dx = hext.shape
    t = min(TILE_DISP, n)
    return pl.pallas_call(
        _dispatch_kernel,
        out_shape=jax.ShapeDtypeStruct((nt_max * TILE_EXP, dx), F32),
        grid_spec=pltpu.PrefetchScalarGridSpec(
            num_scalar_prefetch=3, grid=(n // t,),
            in_specs=[pl.BlockSpec((t, dx), lambda i, *_: (i, 0))],
            out_specs=pl.BlockSpec(memory_space=pl.ANY),
            scratch_shapes=[pltpu.VMEM((TILE_EXP // 2, dx), F32), pltpu.SemaphoreType.DMA((2,))]),
        compiler_params=_params("arbitrary"), name="moe_dispatch",
    )(dest, padstart, padlen, hext)


def _expert_kernel(rowblk_ref, active_ref, ea_ref, eb_ref, hs_ref,
                   wga_ref, wua_ref, wda_ref, wgb_ref, wub_ref, wdb_ref, ys_ref):
    d = ys_ref.shape[1]
    i = pl.program_id(0)

    @pl.when(active_ref[i] == 1)
    def _():
        x = hs_ref[:, :d].astype(BF16)
        y = jnp.zeros(ys_ref.shape, F32)
        for slot, (wg, wu, wd) in enumerate(((wga_ref, wua_ref, wda_ref), (wgb_ref, wub_ref, wdb_ref))):
            hg = jnp.dot(x, wg[0], preferred_element_type=F32)
            ug = jnp.dot(x, wu[0], preferred_element_type=F32)
            act = hg * jax.nn.sigmoid(hg) * ug * hs_ref[:, d + slot:d + slot + 1]
            y = y + jnp.dot(act.astype(BF16), wd[0], preferred_element_type=F32)
        ys_ref[...] = y

    @pl.when(active_ref[i] == 0)
    def _():
        ys_ref[...] = jnp.zeros(ys_ref.shape, F32)


def _experts(hs, rowblk, active, exp_a, exp_b, w_gate, w_up, w_down):
    rows, dx = hs.shape
    d = dx - LANES
    tm = TILE_EXP
    f = w_gate.shape[-1]
    wg = w_gate.reshape(-1, d, f).astype(BF16)
    wu = w_up.reshape(-1, d, f).astype(BF16)
    wd = w_down.reshape(-1, f, d).astype(BF16)
    in_a = lambda shp: pl.BlockSpec((1,) + shp, lambda i, rb, ac, ea, eb: (ea[i], 0, 0))
    in_b = lambda shp: pl.BlockSpec((1,) + shp, lambda i, rb, ac, ea, eb: (eb[i], 0, 0))
    return pl.pallas_call(
        _expert_kernel,
        out_shape=jax.ShapeDtypeStruct((rows, d), F32),
        grid_spec=pltpu.PrefetchScalarGridSpec(
            num_scalar_prefetch=4, grid=(rows // tm,),
            in_specs=[pl.BlockSpec((tm, dx), lambda i, rb, ac, ea, eb: (rb[i], 0)),
                      in_a((d, f)), in_a((d, f)), in_a((f, d)), in_b((d, f)), in_b((d, f)), in_b((f, d))],
            out_specs=pl.BlockSpec((tm, d), lambda i, *_: (i, 0))),
        compiler_params=_params("arbitrary"), name="moe_experts",
    )(rowblk, active, exp_a, exp_b, hs, wg, wu, wd, wg, wu, wd)


def _ple_kernel(dest_ref, x_ref, p_ref, g_ref, wg_ref, wp_ref, fg_ref, ys_ref, o_ref, ybuf, sem, *, final):
    t = x_ref.shape[0]
    i = pl.program_id(0)
    slot = i % 2

    def row_copy(d, r, sl):
        return pltpu.make_async_copy(ys_ref.at[pl.ds(d, 1)], ybuf.at[sl, pl.ds(r, 1)], sem.at[sl])

    def gather(tile, sl):
        def issue(r, carry):
            row_copy(dest_ref[tile * t + r], r, sl).start()
            return carry
        lax.fori_loop(0, t, issue, 0)

    @pl.when(i == 0)
    def _():
        gather(0, 0)

    @pl.when(i + 1 < pl.num_programs(0))
    def _():
        gather(i + 1, 1 - slot)

    def drain(r, carry):
        row_copy(0, 0, slot).wait()
        return carry
    lax.fori_loop(0, t, drain, 0)

    x = x_ref[...] + ybuf[slot]
    gate = jax.nn.sigmoid(jnp.dot(_rms(x, g_ref[...]).astype(BF16), wg_ref[...], preferred_element_type=F32))
    x = x + gate * jnp.dot(p_ref[...].astype(BF16), wp_ref[...], preferred_element_type=F32)
    if final:
        x = _rms(x, fg_ref[...])
    o_ref[...] = x


def _combine_ple(x, ys, dest, p, g, w_gate, w_proj, final_g, final):
    n, d = x.shape
    t = min(TILE_PLE, n)
    full = lambda a: pl.BlockSpec(a.shape, lambda i, *_: (0,) * a.ndim)
    rows = lambda w: pl.BlockSpec((t, w), lambda i, *_: (i, 0))
    args = (g.reshape(1, -1), w_gate.astype(BF16), w_proj.astype(BF16), final_g.reshape(1, -1))
    return pl.pallas_call(
        functools.partial(_ple_kernel, final=final),
        out_shape=jax.ShapeDtypeStruct((n, d), F32),
        grid_spec=pltpu.PrefetchScalarGridSpec(
            num_scalar_prefetch=1, grid=(n // t,),
            in_specs=[rows(d), rows(p.shape[1])] + [full(a) for a in args] + [pl.BlockSpec(memory_space=pl.ANY)],
            out_specs=rows(d),
            scratch_shapes=[pltpu.VMEM((2, t, d), F32), pltpu.SemaphoreType.DMA((2,))]),
        compiler_params=_params("arbitrary"), name="moe_combine_ple",
    )(dest, x, p, *args, ys)


def _moe_ple(x, p, g_ffn, w_rg, b_rg, w_re, b_re, w_gate, w_up, w_down, ple_g, ple_wg, ple_wp, final_g, final,
             attn_o=None, w_o=None):
    n = x.shape[0]
    x, hext, meta, cnt = _route(x, g_ffn, w_rg, b_rg, w_re, b_re, attn_o, w_o)
    dest, rowblk, active, exp_a, exp_b, padstart, padlen, nt_max = _plan(meta, cnt, n)
    hs = _dispatch(hext, dest, padstart, padlen, nt_max)
    ys = _experts(hs, rowblk, active, exp_a, exp_b, w_gate, w_up, w_down)
    return _combine_ple(x, ys, dest, p, ple_g, ple_wg, ple_wp, final_g, final)


def _qkv_kernel(x_ref, pos_ref, g_ref, wq_ref, wk_ref, wv_ref, invf_ref, q_ref, k_ref, vt_ref, *, qscale):
    t = x_ref.shape[1]
    h = _rms(x_ref[0], g_ref[...]).astype(BF16)
    lane = lax.broadcasted_iota(I32, (t, LANES), 1) % HEAD_DIM
    ang = pos_ref[0].astype(F32) * invf_ref[...]
    cos = jnp.cos(ang)
    sin = jnp.sin(ang)
    half = ROPE_DIM // 2
    cmul = jnp.where(lane < ROPE_DIM, cos, 1.0)
    s_up = jnp.where(lane < half, -sin, 0.0)
    s_dn = jnp.where(lane >= half, jnp.where(lane < ROPE_DIM, sin, 0.0), 0.0)

    def rope(w_ref, o_ref, scale):
        y = jnp.dot(h, w_ref[...], preferred_element_type=F32)
        for c0 in range(0, y.shape[1], LANES):
            yb = y[:, c0:c0 + LANES]
            rot = yb * cmul + pltpu.roll(yb, LANES - half, 1) * s_up + pltpu.roll(yb, half, 1) * s_dn
            o_ref[0, :, c0:c0 + LANES] = (rot * scale).astype(o_ref.dtype)

    rope(wq_ref, q_ref, qscale)
    rope(wk_ref, k_ref, 1.0)
    v = jnp.dot(h, wv_ref[...], preferred_element_type=F32)
    vt_ref[0] = v.T.astype(vt_ref.dtype)


def _qkv(x, positions, g, w_qkv):
    b, s, d = x.shape
    t = min(TILE_QKV, s)
    nq = (w_qkv.shape[1] - d) // 2
    w = w_qkv.astype(BF16)
    wq, wk, wv = w[:, :nq], w[:, nq:2 * nq], w[:, 2 * nq:]
    inv_freq = ROPE_THETA ** (-jnp.arange(0, ROPE_DIM, 2, dtype=F32) / ROPE_DIM)
    lane = np.arange(LANES) % HEAD_DIM
    invf = jnp.where(lane < ROPE_DIM, jnp.take(inv_freq, lane % (ROPE_DIM // 2)), 0.0).reshape(1, LANES)
    qscale = HEAD_DIM ** -0.5 * math.log2(math.e)
    full = lambda a: pl.BlockSpec(a.shape, lambda i, j: (0,) * a.ndim)
    args = (g.reshape(1, -1), wq, wk, wv, invf)
    return pl.pallas_call(
        functools.partial(_qkv_kernel, qscale=qscale),
        out_shape=[jax.ShapeDtypeStruct((b, s, nq), BF16), jax.ShapeDtypeStruct((b, s, nq), BF16),
                   jax.ShapeDtypeStruct((b, d, s), BF16)],
        grid=(b, s // t),
        in_specs=[pl.BlockSpec((1, t, d), lambda i, j: (i, j, 0)), pl.BlockSpec((1, t, 1), lambda i, j: (i, j, 0))]
                 + [full(a) for a in args],
        out_specs=[pl.BlockSpec((1, t, nq), lambda i, j: (i, j, 0)), pl.BlockSpec((1, t, nq), lambda i, j: (i, j, 0)),
                   pl.BlockSpec((1, d, t), lambda i, j: (i, 0, j))],
        compiler_params=_params("arbitrary", "arbitrary"), name="qkv_rope",
    )(x, positions.reshape(b, s, 1), *args)


def _attn_kernel(q_ref, k_ref, vt_ref, lam_ref, sg_ref, o_ref, m_scr, l_scr, acc_scr, *, lam_init):
    tq = q_ref.shape[1]
    tk = TILE_K if tq % TILE_K == 0 else tq
    qi = pl.program_id(2)
    q = q_ref[0]
    m_scr[...] = jnp.full(m_scr.shape, -1e30, F32)
    l_scr[...] = jnp.zeros(l_scr.shape, F32)
    acc_scr[...] = jnp.zeros(acc_scr.shape, F32)

    def block(k0, masked):
        kb = k_ref[0, pl.ds(k0, tk), :]
        vtb = vt_ref[0, :, pl.ds(k0, tk)]
        for c in range(2):
            st = lax.dot_general(kb[:, c * HEAD_DIM:(c + 1) * HEAD_DIM], q[:, c * HEAD_DIM:(c + 1) * HEAD_DIM],
                                 (((1,), (1,)), ((), ())), preferred_element_type=F32)
            if masked:
                kpos = k0 + lax.broadcasted_iota(I32, (tk, tq), 0)
                qpos = qi * tq + lax.broadcasted_iota(I32, (tk, tq), 1)
                st = jnp.where(qpos >= kpos, st, -1e30)
            m_old = m_scr[c:c + 1, :]
            m_new = jnp.maximum(m_old, jnp.max(st, axis=0, keepdims=True))
            alpha = jnp.exp2(m_old - m_new)
            p = jnp.exp2(st - m_new)
            l_scr[c:c + 1, :] = alpha * l_scr[c:c + 1, :] + jnp.sum(p, axis=0, keepdims=True)
            m_scr[c:c + 1, :] = m_new
            acc_scr[c] = alpha * acc_scr[c] + jnp.dot(vtb, p.astype(BF16), preferred_element_type=F32)

    def full_block(j, carry):
        block(pl.multiple_of(j * tk, tk), False)
        return carry

    lax.fori_loop(0, qi * (tq // tk), full_block, 0)
    for jj in range(tq // tk):
        block(pl.multiple_of(qi * tq + jj * tk, tk), True)

    lp = lam_ref[...]
    lam = (jnp.exp(jnp.sum(lp[0:1] * lp[1:2], axis=1, keepdims=True))
           - jnp.exp(jnp.sum(lp[2:3] * lp[3:4], axis=1, keepdims=True)) + lam_init)
    ot = acc_scr[0] / l_scr[0:1, :] - lam * (acc_scr[1] / l_scr[1:2, :])
    ot = ot * lax.rsqrt(jnp.mean(ot * ot, axis=0, keepdims=True) + EPS) * sg_ref[...] * (1.0 - lam_init)
    o_ref[0] = ot.T.astype(o_ref.dtype)


def _attention(q, k, vt, lam_params, subln_g, lam_init):
    b, s, nq = q.shape
    heads = nq // (2 * HEAD_DIM)
    tq = min(TILE_Q, s)
    return pl.pallas_call(
        functools.partial(_attn_kernel, lam_init=lam_init),
        out_shape=jax.ShapeDtypeStruct((b, s, heads * V_DIM), BF16),
        grid=(b, heads, s // tq),
        in_specs=[pl.BlockSpec((1, tq, 2 * HEAD_DIM), lambda i, h, j: (i, j, h)),
                  pl.BlockSpec((1, s, 2 * HEAD_DIM), lambda i, h, j: (i, 0, h)),
                  pl.BlockSpec((1, V_DIM, s), lambda i, h, j: (i, h, 0)),
                  pl.BlockSpec(lam_params.shape, lambda i, h, j: (0, 0)),
                  pl.BlockSpec((V_DIM, 1), lambda i, h, j: (0, 0))],
        out_specs=pl.BlockSpec((1, tq, V_DIM), lambda i, h, j: (i, j, h)),
        scratch_shapes=[pltpu.VMEM((2, tq), F32), pltpu.VMEM((2, tq), F32), pltpu.VMEM((2, V_DIM, tq), F32)],
        compiler_params=_params("arbitrary", "arbitrary", "arbitrary"), name="diff_attention",
    )(q, k, vt, lam_params, subln_g.reshape(V_DIM, 1))


def kernel(x, p, positions, norm_mix, norm_ffn, conv_w_pw1, conv_b_pw1, conv_w_dw, conv_b_dw, conv_ln_g, conv_ln_b,
           conv_w_pw2, conv_b_pw2, da_w_qkv, da_lambda, da_subln, da_w_o, moe_w_rg, moe_b_rg, moe_w_re, moe_b_re,
           moe_w_gate, moe_w_up, moe_w_down, ple_norm, ple_w_gate, ple_w_proj, final_norm):
    b, s, d = x.shape
    n = b * s
    depth = norm_mix.shape[0]
    assert depth == 2, "layer 0 is the conv mixer, layer 1 differential attention"
    pf = p.reshape(depth, n, -1)

    def moe(i, xin, final, attn_o=None, w_o=None):
        return _moe_ple(xin, pf[i], norm_ffn[i], moe_w_rg[i], moe_b_rg[i], moe_w_re[i], moe_b_re[i],
                        moe_w_gate[i], moe_w_up[i], moe_w_down[i], ple_norm[i], ple_w_gate[i], ple_w_proj[i],
                        final_norm, final, attn_o, w_o)

    x = _conv_mixer(x, norm_mix[0], conv_w_pw1[0], conv_b_pw1[0], conv_w_dw[0], conv_b_dw[0],
                    conv_ln_g[0], conv_ln_b[0], conv_w_pw2[0], conv_b_pw2[0])
    x = moe(0, x.reshape(n, d), False)

    lam_init = 0.8 - 0.6 * math.exp(-0.3 * 1)
    q, k, vt = _qkv(x.reshape(b, s, d), positions, norm_mix[1], da_w_qkv[0])
    o = _attention(q, k, vt, da_lambda[0], da_subln[0], lam_init)
    x = moe(1, x, True, o.reshape(n, -1), da_w_o[0])
    return x.reshape(b, s, d)
```

```python
import functools
import math

import numpy as np
import jax
import jax.numpy as jnp
from jax import lax
from jax.experimental import pallas as pl
from jax.experimental.pallas import tpu as pltpu

F32 = jnp.float32
BF16 = jnp.bfloat16
I32 = jnp.int32

EPS = 1e-6
CONV_WIDTH = 31
CONV_HALO = 32
HEAD_DIM = 64
V_DIM = 128
ROPE_DIM = 16
ROPE_THETA = 500000.0
N_GROUPS = 4
N_EXP = 8
N_PAIRS = N_EXP * (N_EXP - 1) // 2
N_CLASSES = N_GROUPS * N_PAIRS
LANES = 128
SUBLANES = 8
VMEM_LIMIT = 56 * 1024 * 1024

TILE_CONV = 512
TILE_ROUTE = 512
TILE_DISP = 512
TILE_EXP = 256
TILE_PLE = 512
TILE_QKV = 512
TILE_Q = 512
TILE_K = 256
CONV_ROWS = 64
CONV_LANES = 128


def _pair_tables():
    remaining = list(range(N_EXP))
    order = []
    center, center_slot = 0, 0
    while len(remaining) > 1:
        others = [v for v in remaining if v != center]
        for v in others:
            order.append((center, v) if center_slot == 0 else (v, center))
        remaining.remove(center)
        center, center_slot = others[-1], 1 - center_slot
    assert len(order) == N_PAIRS and len({frozenset(p) for p in order}) == N_PAIRS
    cls = np.zeros((N_EXP * N_EXP,), np.float32)
    first_is_a = np.zeros((N_EXP * N_EXP,), np.float32)
    for idx, (a, b) in enumerate(order):
        cls[a * N_EXP + b] = idx
        cls[b * N_EXP + a] = idx
        first_is_a[a * N_EXP + b] = 1.0
    tbl = np.zeros((8, LANES), np.float32)
    tbl[0, : N_EXP * N_EXP] = cls
    tbl[1, : N_EXP * N_EXP] = first_is_a
    exp_a = np.array([g * N_EXP + a for g in range(N_GROUPS) for (a, _) in order], np.int32)
    exp_b = np.array([g * N_EXP + b for g in range(N_GROUPS) for (_, b) in order], np.int32)
    return tbl, exp_a, exp_b


_PAIR_TBL, _CLASS_EXP_A, _CLASS_EXP_B = _pair_tables()


def _rms(x, g):
    return x * lax.rsqrt(jnp.mean(x * x, axis=-1, keepdims=True) + EPS) * g


def _params(*sem):
    return pltpu.CompilerParams(dimension_semantics=sem, vmem_limit_bytes=VMEM_LIMIT)


def _conv_kernel(x_ref, g_ref, w1_ref, b1_ref, wdw_ref, bdw_ref, lng_ref, lnb_ref, w2_ref, b2_ref,
                 o_ref, vbuf, cbuf):
    t, c = cbuf.shape
    x = x_ref[0]
    h = _rms(x, g_ref[...])
    u = jnp.dot(h.astype(BF16), w1_ref[...], preferred_element_type=F32) + b1_ref[...]
    v = u[:, :c] * jax.nn.sigmoid(u[:, c:])

    @pl.when(pl.program_id(1) == 0)
    def _():
        vbuf[0:CONV_HALO, :] = jnp.zeros((CONV_HALO, c), F32)

    @pl.when(pl.program_id(1) > 0)
    def _():
        vbuf[0:CONV_HALO, :] = vbuf[t:t + CONV_HALO, :]

    vbuf[CONV_HALO:CONV_HALO + t, :] = v

    first = CONV_HALO - (CONV_WIDTH - 1)
    for c0 in range(0, c, CONV_LANES):
        for r0 in range(0, t, CONV_ROWS):
            acc = jnp.zeros((CONV_ROWS, CONV_LANES), F32)
            for k in range(CONV_WIDTH):
                acc = acc + wdw_ref[k:k + 1, c0:c0 + CONV_LANES] * vbuf[r0 + first + k:r0 + first + k + CONV_ROWS,
                                                                        c0:c0 + CONV_LANES]
            cbuf[r0:r0 + CONV_ROWS, c0:c0 + CONV_LANES] = acc + bdw_ref[:, c0:c0 + CONV_LANES]

    y = cbuf[...]
    mu = jnp.mean(y, axis=-1, keepdims=True)
    d = y - mu
    var = jnp.mean(d * d, axis=-1, keepdims=True)
    y = d * lax.rsqrt(var + EPS) * lng_ref[...] + lnb_ref[...]
    y = y * jax.nn.sigmoid(y)
    o_ref[0] = x + jnp.dot(y.astype(BF16), w2_ref[...], preferred_element_type=F32) + b2_ref[...]


def _conv_mixer(x, g, w1, b1, wdw, bdw, lng, lnb, w2, b2):
    b, s, d = x.shape
    c = w2.shape[0]
    t = min(TILE_CONV, s)
    row = lambda a: a.reshape(1, -1)
    full = lambda a: pl.BlockSpec(a.shape, lambda i, j: (0,) * a.ndim)
    args = (row(g), w1.astype(BF16), row(b1), wdw, row(bdw), row(lng), row(lnb), w2.astype(BF16), row(b2))
    return pl.pallas_call(
        _conv_kernel,
        out_shape=jax.ShapeDtypeStruct(x.shape, F32),
        grid=(b, s // t),
        in_specs=[pl.BlockSpec((1, t, d), lambda i, j: (i, j, 0))] + [full(a) for a in args],
        out_specs=pl.BlockSpec((1, t, d), lambda i, j: (i, j, 0)),
        scratch_shapes=[pltpu.VMEM((CONV_HALO + t, c), F32), pltpu.VMEM((t, c), F32)],
        compiler_params=_params("arbitrary", "arbitrary"),
        name="conv_mixer",
    )(x, *args)


def _route_body(x, g_ref, wr_ref, br_ref, tri_ref, tbl_ref, h3_ref, wts_ref, meta_ref, cnt_ref, cnt_scr):
    t, d = x.shape

    @pl.when(pl.program_id(0) == 0)
    def _():
        cnt_scr[...] = jnp.zeros(cnt_scr.shape, F32)

    h = _rms(x, g_ref[...])
    lg = jnp.dot(h.astype(BF16), wr_ref[...], preferred_element_type=F32) + br_ref[...]
    lane = lax.broadcasted_iota(I32, (t, LANES), 1)
    ninf = -jnp.inf

    gmask = lane < N_GROUPS
    gl = jnp.where(gmask, lg, ninf)
    gmax = jnp.max(gl, axis=1, keepdims=True)
    gidx = jnp.min(jnp.where(gl == gmax, lane, LANES), axis=1, keepdims=True)
    gsum = jnp.sum(jnp.where(gmask, jnp.exp(lg - gmax), 0.0), axis=1, keepdims=True)
    gw = 1.0 / gsum

    lo = N_GROUPS + gidx * N_EXP
    el = jnp.where(lane >= lo, jnp.where(lane < lo + N_EXP, lg, ninf), ninf)
    m1 = jnp.max(el, axis=1, keepdims=True)
    i1 = jnp.min(jnp.where(el == m1, lane, LANES), axis=1, keepdims=True)
    el2 = jnp.where(lane == i1, ninf, el)
    m2 = jnp.max(el2, axis=1, keepdims=True)
    i2 = jnp.min(jnp.where(el2 == m2, lane, LANES), axis=1, keepdims=True)
    dlt = jnp.exp(m2 - m1)
    p1 = 1.0 / (1.0 + dlt)
    w1 = p1 * gw
    w2 = dlt * p1 * gw

    q = (i1 - lo) * N_EXP + (i2 - lo)
    qoh = lane == q
    cl = jnp.sum(jnp.where(qoh, tbl_ref[0:1, :], 0.0), axis=1, keepdims=True)
    fa = jnp.sum(jnp.where(qoh, tbl_ref[1:2, :], 0.0), axis=1, keepdims=True)
    cls = gidx * N_PAIRS + cl.astype(I32)
    first_a = fa > 0.5
    wa = jnp.where(first_a, w1, w2)
    wb = jnp.where(first_a, w2, w1)

    oh = lane == cls
    cum = jnp.dot(tri_ref[...], jnp.where(oh, 1.0, 0.0).astype(BF16), preferred_element_type=F32)
    cnt = cnt_scr[...]
    rank = jnp.sum(jnp.where(oh, cum + cnt, 0.0), axis=1, keepdims=True) - 1.0
    cnt = cnt + cum[t - 1:t, :]
    cnt_scr[...] = cnt
    cnt_ref[...] = cnt

    for s in range(d // LANES):
        h3_ref[pl.ds(s, t, stride=SUBLANES), :] = h[:, s * LANES:(s + 1) * LANES]
    lane8 = lax.broadcasted_iota(I32, (t, 8), 1)
    wts_ref[...] = jnp.where(lane8 == 0, wa, jnp.where(lane8 == 1, wb, 0.0))
    meta_ref[...] = jnp.where(lane8 == 0, cls, jnp.where(lane8 == 1, rank.astype(I32), 0))


def _route0_kernel(x_ref, g_ref, wr_ref, br_ref, tri_ref, tbl_ref, h3_ref, wts_ref, meta_ref, cnt_ref, cnt_scr):
    _route_body(x_ref[...], g_ref, wr_ref, br_ref, tri_ref, tbl_ref, h3_ref, wts_ref, meta_ref, cnt_ref, cnt_scr)


def _route1_kernel(x_ref, o_ref, wo_ref, g_ref, wr_ref, br_ref, tri_ref, tbl_ref,
                   xo_ref, h3_ref, wts_ref, meta_ref, cnt_ref, cnt_scr):
    x = x_ref[...] + jnp.dot(o_ref[...], wo_ref[...], preferred_element_type=F32)
    xo_ref[...] = x
    _route_body(x, g_ref, wr_ref, br_ref, tri_ref, tbl_ref, h3_ref, wts_ref, meta_ref, cnt_ref, cnt_scr)


def _route(x, g, w_rg, b_rg, w_re, b_re, attn_o=None, w_o=None):
    n, d = x.shape
    assert d == SUBLANES * LANES, "one (8, 128) tile per token row"
    t = min(TILE_ROUTE, n)
    wr = jnp.zeros((d, LANES), F32)
    wr = wr.at[:, :N_GROUPS].set(w_rg)
    wr = wr.at[:, N_GROUPS:N_GROUPS + N_GROUPS * N_EXP].set(jnp.transpose(w_re, (1, 0, 2)).reshape(d, -1))
    br = jnp.zeros((1, LANES), F32)
    br = br.at[0, :N_GROUPS].set(b_rg)
    br = br.at[0, N_GROUPS:N_GROUPS + N_GROUPS * N_EXP].set(b_re.reshape(-1))
    tri = jnp.asarray(np.tril(np.ones((t, t), np.float32)), BF16)
    tbl = jnp.asarray(_PAIR_TBL)
    full = lambda a: pl.BlockSpec(a.shape, lambda i: (0,) * a.ndim)
    rows = lambda w: pl.BlockSpec((t, w), lambda i: (i, 0))
    common = (g.reshape(1, -1), wr.astype(BF16), br, tri, tbl)
    out_shape = [jax.ShapeDtypeStruct((n * SUBLANES, LANES), F32), jax.ShapeDtypeStruct((n, 8), F32),
                 jax.ShapeDtypeStruct((n, 8), I32), jax.ShapeDtypeStruct((1, LANES), F32)]
    out_specs = [pl.BlockSpec((t * SUBLANES, LANES), lambda i: (i, 0)), rows(8), rows(8),
                 pl.BlockSpec((1, LANES), lambda i: (0, 0))]
    if attn_o is None:
        h3, wts, meta, cnt = pl.pallas_call(
            _route0_kernel, out_shape=out_shape, grid=(n // t,),
            in_specs=[rows(d)] + [full(a) for a in common], out_specs=out_specs,
            scratch_shapes=[pltpu.VMEM((1, LANES), F32)],
            compiler_params=_params("arbitrary"), name="moe_route",
        )(x, *common)
        return x, h3, wts, meta, cnt
    wo = w_o.astype(BF16)
    xo, h3, wts, meta, cnt = pl.pallas_call(
        _route1_kernel, out_shape=[jax.ShapeDtypeStruct((n, d), F32)] + out_shape, grid=(n // t,),
        in_specs=[rows(d), rows(attn_o.shape[1]), full(wo)] + [full(a) for a in common],
        out_specs=[rows(d)] + out_specs,
        scratch_shapes=[pltpu.VMEM((1, LANES), F32)],
        compiler_params=_params("arbitrary"), name="attn_out_moe_route",
    )(x, attn_o, wo, *common)
    return xo, h3, wts, meta, cnt


def _plan(meta, cnt, n):
    tm = TILE_EXP
    nt_max = n // tm + N_CLASSES
    cnt = cnt[0, :N_CLASSES].astype(I32)
    nt = (cnt + tm - 1) // tm
    tend = jnp.cumsum(nt)
    offs = (tend - nt) * tm
    total = tend[-1]
    classes = jnp.arange(N_CLASSES, dtype=I32)
    dest = jnp.sum(jnp.where(meta[:, 0:1] == classes[None, :], offs[None, :], 0), axis=1) + meta[:, 1]
    tile = jnp.arange(nt_max, dtype=I32)
    rowblk = jnp.minimum(tile, total - 1)
    tcls = jnp.sum((tend[None, :] <= rowblk[:, None]).astype(I32), axis=1)
    active = (tile < total).astype(I32)
    exp_a = jnp.take(jnp.asarray(_CLASS_EXP_A), tcls)
    exp_b = jnp.take(jnp.asarray(_CLASS_EXP_B), tcls)
    pad = jnp.zeros((LANES,), I32)
    padstart = pad.at[:N_CLASSES].set(offs + cnt)
    padstart = padstart.at[N_CLASSES].set(total * 2)
    padlen = pad.at[:N_CLASSES].set(nt * tm - cnt)
    return dest, rowblk, active, exp_a, exp_b, padstart, padlen, nt_max


_PAD_BITS = tuple(1 << b for b in reversed(range(int(math.log2(TILE_EXP)))))
DMA_UNROLL = 8


def _dispatch_kernel(dest_ref, padstart_ref, padlen_ref, h3_ref, hs_ref, zbuf, sem):
    rows = h3_ref.shape[0]
    t = rows // SUBLANES
    i = pl.program_id(0)

    def issue(r, carry):
        d = dest_ref[i * t + r]
        pltpu.make_async_copy(h3_ref.at[pl.ds(pl.multiple_of(r * SUBLANES, SUBLANES), SUBLANES)],
                              hs_ref.at[pl.ds(pl.multiple_of(d * SUBLANES, SUBLANES), SUBLANES)], sem.at[0]).start()
        return carry

    lax.fori_loop(0, t, issue, 0, unroll=DMA_UNROLL)
    pltpu.make_async_copy(h3_ref, hs_ref.at[pl.ds(0, rows)], sem.at[0]).wait()

    @pl.when(i == pl.num_programs(0) - 1)
    def _():
        zbuf[...] = jnp.zeros(zbuf.shape, F32)

        def pad_copy(start, size):
            return pltpu.make_async_copy(zbuf.at[pl.ds(0, size * SUBLANES)],
                                         hs_ref.at[pl.ds(pl.multiple_of(start * SUBLANES, SUBLANES), size * SUBLANES)],
                                         sem.at[1])

        def pieces(c, fn):
            start = padstart_ref[c]
            left = padlen_ref[c]
            for bit in _PAD_BITS:
                has = (left & bit) != 0

                @pl.when(has)
                def _(start=start, bit=bit):
                    fn(pad_copy(start, bit))

                start = start + jnp.where(has, bit, 0)

        def issue_pad(c, carry):
            pieces(c, lambda cp: cp.start())
            return carry

        def drain_pad(c, carry):
            pieces(c, lambda cp: cp.wait())
            return carry

        lax.fori_loop(0, N_CLASSES, issue_pad, 0)
        lax.fori_loop(0, N_CLASSES, drain_pad, 0)

        half = zbuf.shape[0] // SUBLANES
        first_unused = padstart_ref[N_CLASSES]
        n_halves = hs_ref.shape[0] // zbuf.shape[0]

        def issue_tail(j, carry):
            pad_copy(j * half, half).start()
            return carry

        def drain_tail(j, carry):
            pad_copy(j * half, half).wait()
            return carry

        lax.fori_loop(first_unused, n_halves, issue_tail, 0)
        lax.fori_loop(first_unused, n_halves, drain_tail, 0)


def _dispatch(h3, dest, padstart, padlen, nt_max):
    n = h3.shape[0] // SUBLANES
    t = min(TILE_DISP, n)
    return pl.pallas_call(
        _dispatch_kernel,
        out_shape=jax.ShapeDtypeStruct((nt_max * TILE_EXP * SUBLANES, LANES), F32),
        grid_spec=pltpu.PrefetchScalarGridSpec(
            num_scalar_prefetch=3, grid=(n // t,),
            in_specs=[pl.BlockSpec((t * SUBLANES, LANES), lambda i, *_: (i, 0))],
            out_specs=pl.BlockSpec(memory_space=pl.ANY),
            scratch_shapes=[pltpu.VMEM((TILE_EXP // 2 * SUBLANES, LANES), F32), pltpu.SemaphoreType.DMA((2,))]),
        compiler_params=_params("arbitrary"), name="moe_dispatch",
    )(dest, padstart, padlen, h3)


def _expert_kernel(rowblk_ref, active_ref, ea_ref, eb_ref, hs_ref,
                   wga_ref, wua_ref, wda_ref, wgb_ref, wub_ref, wdb_ref, ys_ref, rec_scr):
    tm = hs_ref.shape[0] // SUBLANES
    i = pl.program_id(0)

    @pl.when(active_ref[i] == 1)
    def _():
        x = jnp.concatenate([hs_ref[pl.ds(s, tm, stride=SUBLANES), :] for s in range(SUBLANES)], axis=1).astype(BF16)
        for slot, (wg, wu, wd) in enumerate(((wga_ref, wua_ref, wda_ref), (wgb_ref, wub_ref, wdb_ref))):
            hg = jnp.dot(x, wg[0], preferred_element_type=F32)
            ug = jnp.dot(x, wu[0], preferred_element_type=F32)
            act = hg * jax.nn.sigmoid(hg) * ug
            y = jnp.dot(act.astype(BF16), wd[0], preferred_element_type=F32)
            for s in range(SUBLANES):
                rec_scr[pl.ds(slot * SUBLANES + s, tm, stride=2 * SUBLANES), :] = y[:, s * LANES:(s + 1) * LANES]
        ys_ref[...] = rec_scr[...].astype(ys_ref.dtype)

    @pl.when(active_ref[i] == 0)
    def _():
        ys_ref[...] = jnp.zeros(ys_ref.shape, ys_ref.dtype)


def _experts(hs, rowblk, active, exp_a, exp_b, w_gate, w_up, w_down):
    tm = TILE_EXP
    rows = hs.shape[0] // SUBLANES
    d = SUBLANES * LANES
    f = w_gate.shape[-1]
    wg = w_gate.reshape(-1, d, f).astype(BF16)
    wu = w_up.reshape(-1, d, f).astype(BF16)
    wd = w_down.reshape(-1, f, d).astype(BF16)
    in_a = lambda shp: pl.BlockSpec((1,) + shp, lambda i, rb, ac, ea, eb: (ea[i], 0, 0))
    in_b = lambda shp: pl.BlockSpec((1,) + shp, lambda i, rb, ac, ea, eb: (eb[i], 0, 0))
    return pl.pallas_call(
        _expert_kernel,
        out_shape=jax.ShapeDtypeStruct((rows * 2 * SUBLANES, LANES), BF16),
        grid_spec=pltpu.PrefetchScalarGridSpec(
            num_scalar_prefetch=4, grid=(rows // tm,),
            in_specs=[pl.BlockSpec((tm * SUBLANES, LANES), lambda i, rb, ac, ea, eb: (rb[i], 0)),
                      in_a((d, f)), in_a((d, f)), in_a((f, d)), in_b((d, f)), in_b((d, f)), in_b((f, d))],
            out_specs=pl.BlockSpec((tm * 2 * SUBLANES, LANES), lambda i, *_: (i, 0)),
            scratch_shapes=[pltpu.VMEM((tm * 2 * SUBLANES, LANES), F32)]),
        compiler_params=_params("arbitrary"), name="moe_experts",
    )(rowblk, active, exp_a, exp_b, hs, wg, wu, wd, wg, wu, wd)


def _ple_kernel(dest_ref, x_ref, wts_ref, p_ref, g_ref, wg_ref, wp_ref, fg_ref, ys_ref, o_ref, ybuf, rec_scr, sem,
                *, final):
    t = x_ref.shape[0]
    rec = 2 * SUBLANES
    i = pl.program_id(0)
    slot = i % 2

    def gather(tile, sl):
        def issue(r, carry):
            d = dest_ref[tile * t + r]
            pltpu.make_async_copy(ys_ref.at[pl.ds(pl.multiple_of(d * rec, rec), rec)],
                                  ybuf.at[sl, pl.ds(pl.multiple_of(r * rec, rec), rec)], sem.at[sl]).start()
            return carry
        lax.fori_loop(0, t, issue, 0, unroll=DMA_UNROLL)

    @pl.when(i == 0)
    def _():
        gather(0, 0)

    @pl.when(i + 1 < pl.num_programs(0))
    def _():
        gather(i + 1, 1 - slot)

    pltpu.make_async_copy(ys_ref.at[pl.ds(0, t * rec)], ybuf.at[slot], sem.at[slot]).wait()

    rec_scr[...] = ybuf[slot].astype(F32)
    wts = wts_ref[...]
    x = x_ref[...]
    for e in range(2):
        ye = jnp.concatenate([rec_scr[pl.ds(e * SUBLANES + s, t, stride=rec), :] for s in range(SUBLANES)], axis=1)
        x = x + wts[:, e:e + 1] * ye
    gate = jax.nn.sigmoid(jnp.dot(_rms(x, g_ref[...]).astype(BF16), wg_ref[...], preferred_element_type=F32))
    x = x + gate * jnp.dot(p_ref[...].astype(BF16), wp_ref[...], preferred_element_type=F32)
    if final:
        x = _rms(x, fg_ref[...])
    o_ref[...] = x


def _combine_ple(x, ys, dest, wts, p, g, w_gate, w_proj, final_g, final):
    n, d = x.shape
    t = min(TILE_PLE, n)
    rec = 2 * SUBLANES
    full = lambda a: pl.BlockSpec(a.shape, lambda i, *_: (0,) * a.ndim)
    rows = lambda w: pl.BlockSpec((t, w), lambda i, *_: (i, 0))
    args = (g.reshape(1, -1), w_gate.astype(BF16), w_proj.astype(BF16), final_g.reshape(1, -1))
    return pl.pallas_call(
        functools.partial(_ple_kernel, final=final),
        out_shape=jax.ShapeDtypeStruct((n, d), F32),
        grid_spec=pltpu.PrefetchScalarGridSpec(
            num_scalar_prefetch=1, grid=(n // t,),
            in_specs=[rows(d), rows(wts.shape[1]), rows(p.shape[1])] + [full(a) for a in args]
                     + [pl.BlockSpec(memory_space=pl.ANY)],
            out_specs=rows(d),
            scratch_shapes=[pltpu.VMEM((2, t * rec, LANES), ys.dtype), pltpu.VMEM((t * rec, LANES), F32),
                            pltpu.SemaphoreType.DMA((2,))]),
        compiler_params=_params("arbitrary"), name="moe_combine_ple",
    )(dest, x, wts, p, *args, ys)


def _moe_ple(x, p, g_ffn, w_rg, b_rg, w_re, b_re, w_gate, w_up, w_down, ple_g, ple_wg, ple_wp, final_g, final,
             attn_o=None, w_o=None):
    n = x.shape[0]
    x, h3, wts, meta, cnt = _route(x, g_ffn, w_rg, b_rg, w_re, b_re, attn_o, w_o)
    dest, rowblk, active, exp_a, exp_b, padstart, padlen, nt_max = _plan(meta, cnt, n)
    hs = _dispatch(h3, dest, padstart, padlen, nt_max)
    ys = _experts(hs, rowblk, active, exp_a, exp_b, w_gate, w_up, w_down)
    return _combine_ple(x, ys, dest, wts, p, ple_g, ple_wg, ple_wp, final_g, final)


def _qkv_kernel(x_ref, pos_ref, g_ref, wq_ref, wk_ref, wv_ref, invf_ref, q_ref, k_ref, vt_ref, *, qscale):
    t = x_ref.shape[1]
    h = _rms(x_ref[0], g_ref[...]).astype(BF16)
    lane = lax.broadcasted_iota(I32, (t, LANES), 1) % HEAD_DIM
    ang = pos_ref[0].astype(F32) * invf_ref[...]
    cos = jnp.cos(ang)
    sin = jnp.sin(ang)
    half = ROPE_DIM // 2
    cmul = jnp.where(lane < ROPE_DIM, cos, 1.0)
    s_up = jnp.where(lane < half, -sin, 0.0)
    s_dn = jnp.where(lane >= half, jnp.where(lane < ROPE_DIM, sin, 0.0), 0.0)

    def rope(w_ref, o_ref, scale):
        y = jnp.dot(h, w_ref[...], preferred_element_type=F32)
        for c0 in range(0, y.shape[1], LANES):
            yb = y[:, c0:c0 + LANES]
            rot = yb * cmul + pltpu.roll(yb, LANES - half, 1) * s_up + pltpu.roll(yb, half, 1) * s_dn
            o_ref[0, :, c0:c0 + LANES] = (rot * scale).astype(o_ref.dtype)

    rope(wq_ref, q_ref, qscale)
    rope(wk_ref, k_ref, 1.0)
    v = jnp.dot(h, wv_ref[...], preferred_element_type=F32)
    vt_ref[0] = v.T.astype(vt_ref.dtype)


def _qkv(x, positions, g, w_qkv):
    b, s, d = x.shape
    t = min(TILE_QKV, s)
    nq = (w_qkv.shape[1] - d) // 2
    w = w_qkv.astype(BF16)
    wq, wk, wv = w[:, :nq], w[:, nq:2 * nq], w[:, 2 * nq:]
    inv_freq = ROPE_THETA ** (-jnp.arange(0, ROPE_DIM, 2, dtype=F32) / ROPE_DIM)
    lane = np.arange(LANES) % HEAD_DIM
    invf = jnp.where(lane < ROPE_DIM, jnp.take(inv_freq, lane % (ROPE_DIM // 2)), 0.0).reshape(1, LANES)
    qscale = HEAD_DIM ** -0.5 * math.log2(math.e)
    full = lambda a: pl.BlockSpec(a.shape, lambda i, j: (0,) * a.ndim)
    args = (g.reshape(1, -1), wq, wk, wv, invf)
    return pl.pallas_call(
        functools.partial(_qkv_kernel, qscale=qscale),
        out_shape=[jax.ShapeDtypeStruct((b, s, nq), BF16), jax.ShapeDtypeStruct((b, s, nq), BF16),
                   jax.ShapeDtypeStruct((b, d, s), BF16)],
        grid=(b, s // t),
        in_specs=[pl.BlockSpec((1, t, d), lambda i, j: (i, j, 0)), pl.BlockSpec((1, t, 1), lambda i, j: (i, j, 0))]
                 + [full(a) for a in args],
        out_specs=[pl.BlockSpec((1, t, nq), lambda i, j: (i, j, 0)), pl.BlockSpec((1, t, nq), lambda i, j: (i, j, 0)),
                   pl.BlockSpec((1, d, t), lambda i, j: (i, 0, j))],
        compiler_params=_params("arbitrary", "arbitrary"), name="qkv_rope",
    )(x, positions.reshape(b, s, 1), *args)


def _attn_kernel(q_ref, k_ref, vt_ref, lam_ref, sg_ref, o_ref, m_scr, l_scr, acc_scr, *, lam_init):
    tq = q_ref.shape[1]
    tk = tq
    qi = pl.program_id(2)
    q = q_ref[0]
    qs = [q[:, c * HEAD_DIM:(c + 1) * HEAD_DIM] for c in range(2)]
    m_scr[...] = jnp.full(m_scr.shape, -1e30, F32)
    l_scr[...] = jnp.zeros(l_scr.shape, F32)
    acc_scr[...] = jnp.zeros(acc_scr.shape, F32)

    def block(k0, masked):
        kb = k_ref[0, pl.ds(k0, tk), :]
        vtb = vt_ref[0, :, pl.ds(k0, tk)]
        sts = [lax.dot_general(kb[:, c * HEAD_DIM:(c + 1) * HEAD_DIM], qs[c], (((1,), (1,)), ((), ())),
                               preferred_element_type=F32) for c in range(2)]
        if masked:
            keep = lax.broadcasted_iota(I32, (tk, tq), 1) >= lax.broadcasted_iota(I32, (tk, tq), 0)
            sts = [jnp.where(keep, st, -1e30) for st in sts]
        for c, st in enumerate(sts):
            m_old = m_scr[c:c + 1, :]
            m_new = jnp.maximum(m_old, jnp.max(st, axis=0, keepdims=True))
            alpha = jnp.exp2(m_old - m_new)
            p = jnp.exp2(st - m_new)
            l_scr[c:c + 1, :] = alpha * l_scr[c:c + 1, :] + jnp.sum(p, axis=0, keepdims=True)
            m_scr[c:c + 1, :] = m_new
            acc_scr[c] = alpha * acc_scr[c] + jnp.dot(vtb, p.astype(BF16), preferred_element_type=F32)

    def full_block(j, carry):
        block(pl.multiple_of(j * tk, tk), False)
        return carry

    lax.fori_loop(0, qi, full_block, 0)
    block(pl.multiple_of(qi * tq, tq), True)

    lp = lam_ref[...]
    lam = (jnp.exp(jnp.sum(lp[0:1] * lp[1:2], axis=1, keepdims=True))
           - jnp.exp(jnp.sum(lp[2:3] * lp[3:4], axis=1, keepdims=True)) + lam_init)
    ot = acc_scr[0] / l_scr[0:1, :] - lam * (acc_scr[1] / l_scr[1:2, :])
    ot = ot * lax.rsqrt(jnp.mean(ot * ot, axis=0, keepdims=True) + EPS) * sg_ref[...] * (1.0 - lam_init)
    o_ref[0] = ot.T.astype(o_ref.dtype)


def _attention(q, k, vt, lam_params, subln_g, lam_init):
    b, s, nq = q.shape
    heads = nq // (2 * HEAD_DIM)
    tq = min(TILE_Q, s)
    return pl.pallas_call(
        functools.partial(_attn_kernel, lam_init=lam_init),
        out_shape=jax.ShapeDtypeStruct((b, s, heads * V_DIM), BF16),
        grid=(b, heads, s // tq),
        in_specs=[pl.BlockSpec((1, tq, 2 * HEAD_DIM), lambda i, h, j: (i, j, h)),
                  pl.BlockSpec((1, s, 2 * HEAD_DIM), lambda i, h, j: (i, 0, h)),
                  pl.BlockSpec((1, V_DIM, s), lambda i, h, j: (i, h, 0)),
                  pl.BlockSpec(lam_params.shape, lambda i, h, j: (0, 0)),
                  pl.BlockSpec((V_DIM, 1), lambda i, h, j: (0, 0))],
        out_specs=pl.BlockSpec((1, tq, V_DIM), lambda i, h, j: (i, j, h)),
        scratch_shapes=[pltpu.VMEM((2, tq), F32), pltpu.VMEM((2, tq), F32), pltpu.VMEM((2, V_DIM, tq), F32)],
        compiler_params=_params("arbitrary", "arbitrary", "arbitrary"), name="diff_attention",
    )(q, k, vt, lam_params, subln_g.reshape(V_DIM, 1))


def kernel(x, p, positions, norm_mix, norm_ffn, conv_w_pw1, conv_b_pw1, conv_w_dw, conv_b_dw, conv_ln_g, conv_ln_b,
           conv_w_pw2, conv_b_pw2, da_w_qkv, da_lambda, da_subln, da_w_o, moe_w_rg, moe_b_rg, moe_w_re, moe_b_re,
           moe_w_gate, moe_w_up, moe_w_down, ple_norm, ple_w_gate, ple_w_proj, final_norm):
    b, s, d = x.shape
    n = b * s
    depth = norm_mix.shape[0]
    assert depth == 2, "layer 0 is the conv mixer, layer 1 differential attention"
    pf = p.reshape(depth, n, -1)

    def moe(i, xin, final, attn_o=None, w_o=None):
        return _moe_ple(xin, pf[i], norm_ffn[i], moe_w_rg[i], moe_b_rg[i], moe_w_re[i], moe_b_re[i],
                        moe_w_gate[i], moe_w_up[i], moe_w_down[i], ple_norm[i], ple_w_gate[i], ple_w_proj[i],
                        final_norm, final, attn_o, w_o)

    x = _conv_mixer(x, norm_mix[0], conv_w_pw1[0], conv_b_pw1[0], conv_w_dw[0], conv_b_dw[0],
                    conv_ln_g[0], conv_ln_b[0], conv_w_pw2[0], conv_b_pw2[0])
    x = moe(0, x.reshape(n, d), False)

    lam_init = 0.8 - 0.6 * math.exp(-0.3 * 1)
    q, k, vt = _qkv(x.reshape(b, s, d), positions, norm_mix[1], da_w_qkv[0])
    o = _attention(q, k, vt, da_lambda[0], da_subln[0], lam_init)
    x = moe(1, x, True, o.reshape(n, -1), da_w_o[0])
    return x.reshape(b, s, d)
```

```python
import functools
import math

import numpy as np
import jax
import jax.numpy as jnp
from jax import lax
from jax.experimental import pallas as pl
from jax.experimental.pallas import tpu as pltpu

F32 = jnp.float32
BF16 = jnp.bfloat16
I32 = jnp.int32

EPS = 1e-6
CONV_WIDTH = 31
CONV_HALO = 32
HEAD_DIM = 64
V_DIM = 128
ROPE_DIM = 16
ROPE_THETA = 500000.0
N_GROUPS = 4
N_EXP = 8
N_PAIRS = N_EXP * (N_EXP - 1) // 2
N_CLASSES = N_GROUPS * N_PAIRS
LANES = 128
SUBLANES = 8
VMEM_LIMIT = 56 * 1024 * 1024

TILE_CONV = 512
TILE_ROUTE = 512
TILE_DISP = 512
TILE_EXP = 256
TILE_PLE = 512
TILE_QKV = 512
TILE_Q = 512
TILE_K = 256
CONV_GROUP = 4
CONV_GAP = 4


def _pair_tables():
    remaining = list(range(N_EXP))
    order = []
    center, center_slot = 0, 0
    while len(remaining) > 1:
        others = [v for v in remaining if v != center]
        for v in others:
            order.append((center, v) if center_slot == 0 else (v, center))
        remaining.remove(center)
        center, center_slot = others[-1], 1 - center_slot
    assert len(order) == N_PAIRS and len({frozenset(p) for p in order}) == N_PAIRS
    cls = np.zeros((N_EXP * N_EXP,), np.float32)
    first_is_a = np.zeros((N_EXP * N_EXP,), np.float32)
    for idx, (a, b) in enumerate(order):
        cls[a * N_EXP + b] = idx
        cls[b * N_EXP + a] = idx
        first_is_a[a * N_EXP + b] = 1.0
    tbl = np.zeros((8, LANES), np.float32)
    tbl[0, : N_EXP * N_EXP] = cls
    tbl[1, : N_EXP * N_EXP] = first_is_a
    exp_a = np.array([g * N_EXP + a for g in range(N_GROUPS) for (a, _) in order], np.int32)
    exp_b = np.array([g * N_EXP + b for g in range(N_GROUPS) for (_, b) in order], np.int32)
    return tbl, exp_a, exp_b


_PAIR_TBL, _CLASS_EXP_A, _CLASS_EXP_B = _pair_tables()


def _rms(x, g):
    return x * lax.rsqrt(jnp.mean(x * x, axis=-1, keepdims=True) + EPS) * g


def _params(*sem):
    return pltpu.CompilerParams(dimension_semantics=sem, vmem_limit_bytes=VMEM_LIMIT)


def _conv_kernel(x_ref, g_ref, w1_ref, b1_ref, w3_ref, bdw_ref, lng_ref, lnb_ref, w2_ref, b2_ref,
                 o_ref, vbuf, cbuf):
    nslab, t, _ = cbuf.shape
    c = nslab * LANES
    seg = t // SUBLANES
    pitch = CONV_HALO + seg + CONV_GAP
    x = x_ref[0]
    h = _rms(x, g_ref[...])
    u = jnp.dot(h.astype(BF16), w1_ref[...], preferred_element_type=F32) + b1_ref[...]
    v = u[:, :c] * jax.nn.sigmoid(u[:, c:])

    last = (SUBLANES - 1) * pitch + seg
    @pl.when(pl.program_id(1) == 0)
    def _():
        vbuf[:, 0:CONV_HALO, :] = jnp.zeros((nslab, CONV_HALO, LANES), F32)

    @pl.when(pl.program_id(1) > 0)
    def _():
        vbuf[:, 0:CONV_HALO, :] = vbuf[:, last:last + CONV_HALO, :]

    for s in range(nslab):
        vs = v[:, s * LANES:(s + 1) * LANES]
        for j in range(SUBLANES):
            if j > 0:
                vbuf[s, j * pitch:j * pitch + CONV_HALO, :] = vs[j * seg - CONV_HALO:j * seg]
            vbuf[s, j * pitch + CONV_HALO:j * pitch + CONV_HALO + seg, :] = vs[j * seg:(j + 1) * seg]

    first = CONV_HALO - (CONV_WIDTH - 1)
    for s in range(nslab):
        wv = [jnp.broadcast_to(w3_ref[s, k:k + 1, :], (SUBLANES, LANES)) for k in range(CONV_WIDTH)]
        bias = jnp.broadcast_to(bdw_ref[:, s * LANES:(s + 1) * LANES], (SUBLANES, LANES))

        def step(g, carry, s=s, wv=wv, bias=bias):
            i0 = g * CONV_GROUP
            accs = [[bias, None] for _ in range(CONV_GROUP)]
            for off in range(CONV_GROUP + CONV_WIDTH - 1):
                val = vbuf[s, pl.ds(i0 + first + off, SUBLANES, stride=pitch), :]
                for u in range(CONV_GROUP):
                    k = off - u
                    if 0 <= k < CONV_WIDTH:
                        prod = wv[k] * val
                        accs[u][k % 2] = prod if accs[u][k % 2] is None else accs[u][k % 2] + prod
            for u in range(CONV_GROUP):
                cbuf[s, pl.ds(pl.multiple_of((i0 + u) * SUBLANES, SUBLANES), SUBLANES), :] = accs[u][0] + accs[u][1]
            return carry

        lax.fori_loop(0, seg // CONV_GROUP, step, 0)

    y = jnp.concatenate(
        [jnp.concatenate([cbuf[s, pl.ds(j, seg, stride=SUBLANES), :] for j in range(SUBLANES)], axis=0)
         for s in range(nslab)], axis=1)
    mu = jnp.mean(y, axis=-1, keepdims=True)
    d = y - mu
    var = jnp.mean(d * d, axis=-1, keepdims=True)
    y = d * lax.rsqrt(var + EPS) * lng_ref[...] + lnb_ref[...]
    y = y * jax.nn.sigmoid(y)
    o_ref[0] = x + jnp.dot(y.astype(BF16), w2_ref[...], preferred_element_type=F32) + b2_ref[...]


def _conv_mixer(x, g, w1, b1, wdw, bdw, lng, lnb, w2, b2):
    b, s, d = x.shape
    c = w2.shape[0]
    t = min(TILE_CONV, s)
    seg = t // SUBLANES
    assert seg >= CONV_HALO and seg % SUBLANES == 0 and c % LANES == 0
    nslab = c // LANES
    pitch = CONV_HALO + seg + CONV_GAP
    row = lambda a: a.reshape(1, -1)
    full = lambda a: pl.BlockSpec(a.shape, lambda i, j: (0,) * a.ndim)
    w3 = jnp.pad(wdw, ((0, CONV_HALO - CONV_WIDTH), (0, 0))).reshape(CONV_HALO, nslab, LANES).transpose(1, 0, 2)
    args = (row(g), w1.astype(BF16), row(b1), w3, row(bdw), row(lng), row(lnb), w2.astype(BF16), row(b2))
    return pl.pallas_call(
        _conv_kernel,
        out_shape=jax.ShapeDtypeStruct(x.shape, F32),
        grid=(b, s // t),
        in_specs=[pl.BlockSpec((1, t, d), lambda i, j: (i, j, 0))] + [full(a) for a in args],
        out_specs=pl.BlockSpec((1, t, d), lambda i, j: (i, j, 0)),
        scratch_shapes=[pltpu.VMEM((nslab, SUBLANES * pitch, LANES), F32), pltpu.VMEM((nslab, t, LANES), F32)],
        compiler_params=_params("arbitrary", "arbitrary"),
        name="conv_mixer",
    )(x, *args)


def _route_body(x, g_ref, wr_ref, br_ref, tri_ref, tbl_ref, h3_ref, wts_ref, meta_ref, cnt_ref, cnt_scr):
    t, d = x.shape

    @pl.when(pl.program_id(0) == 0)
    def _():
        cnt_scr[...] = jnp.zeros(cnt_scr.shape, F32)

    h = _rms(x, g_ref[...])
    lg = jnp.dot(h.astype(BF16), wr_ref[...], preferred_element_type=F32) + br_ref[...]
    lane = lax.broadcasted_iota(I32, (t, LANES), 1)
    ninf = -jnp.inf

    gmask = lane < N_GROUPS
    gl = jnp.where(gmask, lg, ninf)
    gmax = jnp.max(gl, axis=1, keepdims=True)
    gidx = jnp.min(jnp.where(gl == gmax, lane, LANES), axis=1, keepdims=True)
    gsum = jnp.sum(jnp.where(gmask, jnp.exp(lg - gmax), 0.0), axis=1, keepdims=True)
    gw = 1.0 / gsum

    lo = N_GROUPS + gidx * N_EXP
    el = jnp.where(lane >= lo, jnp.where(lane < lo + N_EXP, lg, ninf), ninf)
    m1 = jnp.max(el, axis=1, keepdims=True)
    i1 = jnp.min(jnp.where(el == m1, lane, LANES), axis=1, keepdims=True)
    el2 = jnp.where(lane == i1, ninf, el)
    m2 = jnp.max(el2, axis=1, keepdims=True)
    i2 = jnp.min(jnp.where(el2 == m2, lane, LANES), axis=1, keepdims=True)
    dlt = jnp.exp(m2 - m1)
    p1 = 1.0 / (1.0 + dlt)
    w1 = p1 * gw
    w2 = dlt * p1 * gw

    q = (i1 - lo) * N_EXP + (i2 - lo)
    qoh = lane == q
    cl = jnp.sum(jnp.where(qoh, tbl_ref[0:1, :], 0.0), axis=1, keepdims=True)
    fa = jnp.sum(jnp.where(qoh, tbl_ref[1:2, :], 0.0), axis=1, keepdims=True)
    cls = gidx * N_PAIRS + cl.astype(I32)
    first_a = fa > 0.5
    wa = jnp.where(first_a, w1, w2)
    wb = jnp.where(first_a, w2, w1)

    oh = lane == cls
    cum = jnp.dot(tri_ref[...], jnp.where(oh, 1.0, 0.0).astype(BF16), preferred_element_type=F32)
    cnt = cnt_scr[...]
    rank = jnp.sum(jnp.where(oh, cum + cnt, 0.0), axis=1, keepdims=True) - 1.0
    cnt = cnt + cum[t - 1:t, :]
    cnt_scr[...] = cnt
    cnt_ref[...] = cnt

    for s in range(d // LANES):
        h3_ref[pl.ds(s, t, stride=SUBLANES), :] = h[:, s * LANES:(s + 1) * LANES]
    lane8 = lax.broadcasted_iota(I32, (t, 8), 1)
    wts_ref[...] = jnp.where(lane8 == 0, wa, jnp.where(lane8 == 1, wb, 0.0))
    meta_ref[...] = jnp.where(lane8 == 0, cls, jnp.where(lane8 == 1, rank.astype(I32), 0))


def _route0_kernel(x_ref, g_ref, wr_ref, br_ref, tri_ref, tbl_ref, h3_ref, wts_ref, meta_ref, cnt_ref, cnt_scr):
    _route_body(x_ref[...], g_ref, wr_ref, br_ref, tri_ref, tbl_ref, h3_ref, wts_ref, meta_ref, cnt_ref, cnt_scr)


def _route1_kernel(x_ref, o_ref, wo_ref, g_ref, wr_ref, br_ref, tri_ref, tbl_ref,
                   xo_ref, h3_ref, wts_ref, meta_ref, cnt_ref, cnt_scr):
    x = x_ref[...] + jnp.dot(o_ref[...], wo_ref[...], preferred_element_type=F32)
    xo_ref[...] = x
    _route_body(x, g_ref, wr_ref, br_ref, tri_ref, tbl_ref, h3_ref, wts_ref, meta_ref, cnt_ref, cnt_scr)


def _route(x, g, w_rg, b_rg, w_re, b_re, attn_o=None, w_o=None):
    n, d = x.shape
    assert d == SUBLANES * LANES, "one (8, 128) tile per token row"
    t = min(TILE_ROUTE, n)
    wr = jnp.zeros((d, LANES), F32)
    wr = wr.at[:, :N_GROUPS].set(w_rg)
    wr = wr.at[:, N_GROUPS:N_GROUPS + N_GROUPS * N_EXP].set(jnp.transpose(w_re, (1, 0, 2)).reshape(d, -1))
    br = jnp.zeros((1, LANES), F32)
    br = br.at[0, :N_GROUPS].set(b_rg)
    br = br.at[0, N_GROUPS:N_GROUPS + N_GROUPS * N_EXP].set(b_re.reshape(-1))
    tri = jnp.asarray(np.tril(np.ones((t, t), np.float32)), BF16)
    tbl = jnp.asarray(_PAIR_TBL)
    full = lambda a: pl.BlockSpec(a.shape, lambda i: (0,) * a.ndim)
    rows = lambda w: pl.BlockSpec((t, w), lambda i: (i, 0))
    common = (g.reshape(1, -1), wr.astype(BF16), br, tri, tbl)
    out_shape = [jax.ShapeDtypeStruct((n * SUBLANES, LANES), F32), jax.ShapeDtypeStruct((n, 8), F32),
                 jax.ShapeDtypeStruct((n, 8), I32), jax.ShapeDtypeStruct((1, LANES), F32)]
    out_specs = [pl.BlockSpec((t * SUBLANES, LANES), lambda i: (i, 0)), rows(8), rows(8),
                 pl.BlockSpec((1, LANES), lambda i: (0, 0))]
    if attn_o is None:
        h3, wts, meta, cnt = pl.pallas_call(
            _route0_kernel, out_shape=out_shape, grid=(n // t,),
            in_specs=[rows(d)] + [full(a) for a in common], out_specs=out_specs,
            scratch_shapes=[pltpu.VMEM((1, LANES), F32)],
            compiler_params=_params("arbitrary"), name="moe_route",
        )(x, *common)
        return x, h3, wts, meta, cnt
    wo = w_o.astype(BF16)
    xo, h3, wts, meta, cnt = pl.pallas_call(
        _route1_kernel, out_shape=[jax.ShapeDtypeStruct((n, d), F32)] + out_shape, grid=(n // t,),
        in_specs=[rows(d), rows(attn_o.shape[1]), full(wo)] + [full(a) for a in common],
        out_specs=[rows(d)] + out_specs,
        scratch_shapes=[pltpu.VMEM((1, LANES), F32)],
        compiler_params=_params("arbitrary"), name="attn_out_moe_route",
    )(x, attn_o, wo, *common)
    return xo, h3, wts, meta, cnt


def _plan(meta, cnt, n):
    tm = TILE_EXP
    nt_max = n // tm + N_CLASSES
    cnt = cnt[0, :N_CLASSES].astype(I32)
    nt = (cnt + tm - 1) // tm
    tend = jnp.cumsum(nt)
    offs = (tend - nt) * tm
    total = tend[-1]
    classes = jnp.arange(N_CLASSES, dtype=I32)
    dest = jnp.sum(jnp.where(meta[:, 0:1] == classes[None, :], offs[None, :], 0), axis=1) + meta[:, 1]
    tile = jnp.arange(nt_max, dtype=I32)
    rowblk = jnp.minimum(tile, total - 1)
    tcls = jnp.sum((tend[None, :] <= rowblk[:, None]).astype(I32), axis=1)
    active = (tile < total).astype(I32)
    exp_a = jnp.take(jnp.asarray(_CLASS_EXP_A), tcls)
    exp_b = jnp.take(jnp.asarray(_CLASS_EXP_B), tcls)
    pad = jnp.zeros((LANES,), I32)
    padstart = pad.at[:N_CLASSES].set(offs + cnt)
    padstart = padstart.at[N_CLASSES].set(total * 2)
    padlen = pad.at[:N_CLASSES].set(nt * tm - cnt)
    return dest, rowblk, active, exp_a, exp_b, padstart, padlen, nt_max


_PAD_BITS = tuple(1 << b for b in reversed(range(int(math.log2(TILE_EXP)))))
DMA_UNROLL = 8


def _dispatch_kernel(dest_ref, padstart_ref, padlen_ref, h3_ref, hs_ref, zbuf, sem):
    rows = h3_ref.shape[0]
    t = rows // SUBLANES
    i = pl.program_id(0)

    def issue(r, carry):
        d = dest_ref[i * t + r]
        pltpu.make_async_copy(h3_ref.at[pl.ds(pl.multiple_of(r * SUBLANES, SUBLANES), SUBLANES)],
                              hs_ref.at[pl.ds(pl.multiple_of(d * SUBLANES, SUBLANES), SUBLANES)], sem.at[0]).start()
        return carry

    lax.fori_loop(0, t, issue, 0, unroll=DMA_UNROLL)
    pltpu.make_async_copy(h3_ref, hs_ref.at[pl.ds(0, rows)], sem.at[0]).wait()

    @pl.when(i == pl.num_programs(0) - 1)
    def _():
        zbuf[...] = jnp.zeros(zbuf.shape, F32)

        def pad_copy(start, size):
            return pltpu.make_async_copy(zbuf.at[pl.ds(0, size * SUBLANES)],
                                         hs_ref.at[pl.ds(pl.multiple_of(start * SUBLANES, SUBLANES), size * SUBLANES)],
                                         sem.at[1])

        def pieces(c, fn):
            start = padstart_ref[c]
            left = padlen_ref[c]
            for bit in _PAD_BITS:
                has = (left & bit) != 0

                @pl.when(has)
                def _(start=start, bit=bit):
                    fn(pad_copy(start, bit))

                start = start + jnp.where(has, bit, 0)

        def issue_pad(c, carry):
            pieces(c, lambda cp: cp.start())
            return carry

        def drain_pad(c, carry):
            pieces(c, lambda cp: cp.wait())
            return carry

        lax.fori_loop(0, N_CLASSES, issue_pad, 0)
        lax.fori_loop(0, N_CLASSES, drain_pad, 0)

        half = zbuf.shape[0] // SUBLANES
        first_unused = padstart_ref[N_CLASSES]
        n_halves = hs_ref.shape[0] // zbuf.shape[0]

        def issue_tail(j, carry):
            pad_copy(j * half, half).start()
            return carry

        def drain_tail(j, carry):
            pad_copy(j * half, half).wait()
            return carry

        lax.fori_loop(first_unused, n_halves, issue_tail, 0)
        lax.fori_loop(first_unused, n_halves, drain_tail, 0)


def _dispatch(h3, dest, padstart, padlen, nt_max):
    n = h3.shape[0] // SUBLANES
    t = min(TILE_DISP, n)
    return pl.pallas_call(
        _dispatch_kernel,
        out_shape=jax.ShapeDtypeStruct((nt_max * TILE_EXP * SUBLANES, LANES), F32),
        grid_spec=pltpu.PrefetchScalarGridSpec(
            num_scalar_prefetch=3, grid=(n // t,),
            in_specs=[pl.BlockSpec((t * SUBLANES, LANES), lambda i, *_: (i, 0))],
            out_specs=pl.BlockSpec(memory_space=pl.ANY),
            scratch_shapes=[pltpu.VMEM((TILE_EXP // 2 * SUBLANES, LANES), F32), pltpu.SemaphoreType.DMA((2,))]),
        compiler_params=_params("arbitrary"), name="moe_dispatch",
    )(dest, padstart, padlen, h3)


def _expert_kernel(rowblk_ref, active_ref, ea_ref, eb_ref, hs_ref,
                   wga_ref, wua_ref, wda_ref, wgb_ref, wub_ref, wdb_ref, ys_ref, rec_scr):
    tm = hs_ref.shape[0] // SUBLANES
    i = pl.program_id(0)

    @pl.when(active_ref[i] == 1)
    def _():
        x = jnp.concatenate([hs_ref[pl.ds(s, tm, stride=SUBLANES), :] for s in range(SUBLANES)], axis=1).astype(BF16)
        for slot, (wg, wu, wd) in enumerate(((wga_ref, wua_ref, wda_ref), (wgb_ref, wub_ref, wdb_ref))):
            hg = jnp.dot(x, wg[0], preferred_element_type=F32)
            ug = jnp.dot(x, wu[0], preferred_element_type=F32)
            act = hg * jax.nn.sigmoid(hg) * ug
            y = jnp.dot(act.astype(BF16), wd[0], preferred_element_type=F32)
            for s in range(SUBLANES):
                rec_scr[slot, pl.ds(s, tm, stride=SUBLANES), :] = y[:, s * LANES:(s + 1) * LANES]
        rec = jnp.stack([rec_scr[0].reshape(tm, SUBLANES, LANES), rec_scr[1].reshape(tm, SUBLANES, LANES)], axis=1)
        ys_ref[...] = rec.reshape(tm * 2 * SUBLANES, LANES).astype(ys_ref.dtype)

    @pl.when(active_ref[i] == 0)
    def _():
        ys_ref[...] = jnp.zeros(ys_ref.shape, ys_ref.dtype)


def _experts(hs, rowblk, active, exp_a, exp_b, w_gate, w_up, w_down):
    tm = TILE_EXP
    rows = hs.shape[0] // SUBLANES
    d = SUBLANES * LANES
    f = w_gate.shape[-1]
    wg = w_gate.reshape(-1, d, f).astype(BF16)
    wu = w_up.reshape(-1, d, f).astype(BF16)
    wd = w_down.reshape(-1, f, d).astype(BF16)
    in_a = lambda shp: pl.BlockSpec((1,) + shp, lambda i, rb, ac, ea, eb: (ea[i], 0, 0))
    in_b = lambda shp: pl.BlockSpec((1,) + shp, lambda i, rb, ac, ea, eb: (eb[i], 0, 0))
    return pl.pallas_call(
        _expert_kernel,
        out_shape=jax.ShapeDtypeStruct((rows * 2 * SUBLANES, LANES), BF16),
        grid_spec=pltpu.PrefetchScalarGridSpec(
            num_scalar_prefetch=4, grid=(rows // tm,),
            in_specs=[pl.BlockSpec((tm * SUBLANES, LANES), lambda i, rb, ac, ea, eb: (rb[i], 0)),
                      in_a((d, f)), in_a((d, f)), in_a((f, d)), in_b((d, f)), in_b((d, f)), in_b((f, d))],
            out_specs=pl.BlockSpec((tm * 2 * SUBLANES, LANES), lambda i, *_: (i, 0)),
            scratch_shapes=[pltpu.VMEM((2, tm * SUBLANES, LANES), F32)]),
        compiler_params=_params("arbitrary"), name="moe_experts",
    )(rowblk, active, exp_a, exp_b, hs, wg, wu, wd, wg, wu, wd)


def _ple_kernel(dest_ref, x_ref, wts_ref, p_ref, g_ref, wg_ref, wp_ref, fg_ref, ys_ref, o_ref, ybuf, rec_scr, sem,
                *, final):
    t = x_ref.shape[0]
    rec = 2 * SUBLANES
    i = pl.program_id(0)
    slot = i % 2

    def gather(tile, sl):
        def issue(r, carry):
            d = dest_ref[tile * t + r]
            pltpu.make_async_copy(ys_ref.at[pl.ds(pl.multiple_of(d * rec, rec), rec)],
                                  ybuf.at[sl, pl.ds(pl.multiple_of(r * rec, rec), rec)], sem.at[sl]).start()
            return carry
        lax.fori_loop(0, t, issue, 0, unroll=DMA_UNROLL)

    @pl.when(i == 0)
    def _():
        gather(0, 0)

    @pl.when(i + 1 < pl.num_programs(0))
    def _():
        gather(i + 1, 1 - slot)

    pltpu.make_async_copy(ys_ref.at[pl.ds(0, t * rec)], ybuf.at[slot], sem.at[slot]).wait()

    recs = ybuf[slot].astype(F32).reshape(t, 2, SUBLANES, LANES)
    wts = wts_ref[...]
    x = x_ref[...]
    for e in range(2):
        rec_scr[e] = recs[:, e].reshape(t * SUBLANES, LANES)
        ye = jnp.concatenate([rec_scr[e, pl.ds(s, t, stride=SUBLANES), :] for s in range(SUBLANES)], axis=1)
        x = x + wts[:, e:e + 1] * ye
    gate = jax.nn.sigmoid(jnp.dot(_rms(x, g_ref[...]).astype(BF16), wg_ref[...], preferred_element_type=F32))
    x = x + gate * jnp.dot(p_ref[...].astype(BF16), wp_ref[...], preferred_element_type=F32)
    if final:
        x = _rms(x, fg_ref[...])
    o_ref[...] = x


def _combine_ple(x, ys, dest, wts, p, g, w_gate, w_proj, final_g, final):
    n, d = x.shape
    t = min(TILE_PLE, n)
    rec = 2 * SUBLANES
    full = lambda a: pl.BlockSpec(a.shape, lambda i, *_: (0,) * a.ndim)
    rows = lambda w: pl.BlockSpec((t, w), lambda i, *_: (i, 0))
    args = (g.reshape(1, -1), w_gate.astype(BF16), w_proj.astype(BF16), final_g.reshape(1, -1))
    return pl.pallas_call(
        functools.partial(_ple_kernel, final=final),
        out_shape=jax.ShapeDtypeStruct((n, d), F32),
        grid_spec=pltpu.PrefetchScalarGridSpec(
            num_scalar_prefetch=1, grid=(n // t,),
            in_specs=[rows(d), rows(wts.shape[1]), rows(p.shape[1])] + [full(a) for a in args]
                     + [pl.BlockSpec(memory_space=pl.ANY)],
            out_specs=rows(d),
            scratch_shapes=[pltpu.VMEM((2, t * rec, LANES), ys.dtype), pltpu.VMEM((2, t * SUBLANES, LANES), F32),
                            pltpu.SemaphoreType.DMA((2,))]),
        compiler_params=_params("arbitrary"), name="moe_combine_ple",
    )(dest, x, wts, p, *args, ys)


def _moe_ple(x, p, g_ffn, w_rg, b_rg, w_re, b_re, w_gate, w_up, w_down, ple_g, ple_wg, ple_wp, final_g, final,
             attn_o=None, w_o=None):
    n = x.shape[0]
    x, h3, wts, meta, cnt = _route(x, g_ffn, w_rg, b_rg, w_re, b_re, attn_o, w_o)
    dest, rowblk, active, exp_a, exp_b, padstart, padlen, nt_max = _plan(meta, cnt, n)
    hs = _dispatch(h3, dest, padstart, padlen, nt_max)
    ys = _experts(hs, rowblk, active, exp_a, exp_b, w_gate, w_up, w_down)
    return _combine_ple(x, ys, dest, wts, p, ple_g, ple_wg, ple_wp, final_g, final)


def _qkv_kernel(x_ref, pos_ref, g_ref, wq_ref, wk_ref, wv_ref, invf_ref, q_ref, k_ref, vt_ref, *, qscale):
    t = x_ref.shape[1]
    h = _rms(x_ref[0], g_ref[...]).astype(BF16)
    lane = lax.broadcasted_iota(I32, (t, LANES), 1) % HEAD_DIM
    ang = pos_ref[0].astype(F32) * invf_ref[...]
    cos = jnp.cos(ang)
    sin = jnp.sin(ang)
    half = ROPE_DIM // 2
    cmul = jnp.where(lane < ROPE_DIM, cos, 1.0)
    s_up = jnp.where(lane < half, -sin, 0.0)
    s_dn = jnp.where(lane >= half, jnp.where(lane < ROPE_DIM, sin, 0.0), 0.0)

    def rope(w_ref, o_ref, scale):
        y = jnp.dot(h, w_ref[...], preferred_element_type=F32)
        for c0 in range(0, y.shape[1], LANES):
            yb = y[:, c0:c0 + LANES]
            rot = yb * cmul + pltpu.roll(yb, LANES - half, 1) * s_up + pltpu.roll(yb, half, 1) * s_dn
            o_ref[0, :, c0:c0 + LANES] = (rot * scale).astype(o_ref.dtype)

    rope(wq_ref, q_ref, qscale)
    rope(wk_ref, k_ref, 1.0)
    v = jnp.dot(h, wv_ref[...], preferred_element_type=F32)
    vt_ref[0] = v.T.astype(vt_ref.dtype)


def _qkv(x, positions, g, w_qkv):
    b, s, d = x.shape
    t = min(TILE_QKV, s)
    nq = (w_qkv.shape[1] - d) // 2
    w = w_qkv.astype(BF16)
    wq, wk, wv = w[:, :nq], w[:, nq:2 * nq], w[:, 2 * nq:]
    inv_freq = ROPE_THETA ** (-jnp.arange(0, ROPE_DIM, 2, dtype=F32) / ROPE_DIM)
    lane = np.arange(LANES) % HEAD_DIM
    invf = jnp.where(lane < ROPE_DIM, jnp.take(inv_freq, lane % (ROPE_DIM // 2)), 0.0).reshape(1, LANES)
    qscale = HEAD_DIM ** -0.5 * math.log2(math.e)
    full = lambda a: pl.BlockSpec(a.shape, lambda i, j: (0,) * a.ndim)
    args = (g.reshape(1, -1), wq, wk, wv, invf)
    return pl.pallas_call(
        functools.partial(_qkv_kernel, qscale=qscale),
        out_shape=[jax.ShapeDtypeStruct((b, s, nq), BF16), jax.ShapeDtypeStruct((b, s, nq), BF16),
                   jax.ShapeDtypeStruct((b, d, s), BF16)],
        grid=(b, s // t),
        in_specs=[pl.BlockSpec((1, t, d), lambda i, j: (i, j, 0)), pl.BlockSpec((1, t, 1), lambda i, j: (i, j, 0))]
                 + [full(a) for a in args],
        out_specs=[pl.BlockSpec((1, t, nq), lambda i, j: (i, j, 0)), pl.BlockSpec((1, t, nq), lambda i, j: (i, j, 0)),
                   pl.BlockSpec((1, d, t), lambda i, j: (i, 0, j))],
        compiler_params=_params("arbitrary", "arbitrary"), name="qkv_rope",
    )(x, positions.reshape(b, s, 1), *args)


def _attn_kernel(q_ref, k_ref, vt_ref, lam_ref, sg_ref, o_ref, m_scr, l_scr, acc_scr, st_a, st_b, *, lam_init):
    tq = q_ref.shape[1]
    tk = tq
    qi = pl.program_id(2)
    q = q_ref[0]
    qs = [q[:, c * HEAD_DIM:(c + 1) * HEAD_DIM] for c in range(2)]
    m_scr[...] = jnp.full(m_scr.shape, -1e30, F32)
    l_scr[...] = jnp.zeros(l_scr.shape, F32)
    acc_scr[...] = jnp.zeros(acc_scr.shape, F32)

    def scores(buf, j):
        kb = k_ref[0, pl.ds(pl.multiple_of(j * tk, tk), tk), :]
        for c in range(2):
            buf[c] = lax.dot_general(kb[:, c * HEAD_DIM:(c + 1) * HEAD_DIM], qs[c], (((1,), (1,)), ((), ())),
                                     preferred_element_type=F32)

    def accumulate(buf, j, masked):
        vtb = vt_ref[0, :, pl.ds(pl.multiple_of(j * tk, tk), tk)]
        if masked:
            keep = lax.broadcasted_iota(I32, (tk, tq), 1) >= lax.broadcasted_iota(I32, (tk, tq), 0)
        for c in range(2):
            st = buf[c]
            if masked:
                st = jnp.where(keep, st, -1e30)
            m_old = m_scr[c:c + 1, :]
            m_new = jnp.maximum(m_old, jnp.max(st, axis=0, keepdims=True))
            alpha = jnp.exp2(m_old - m_new)
            p = jnp.exp2(st - m_new)
            l_scr[c:c + 1, :] = alpha * l_scr[c:c + 1, :] + jnp.sum(p, axis=0, keepdims=True)
            m_scr[c:c + 1, :] = m_new
            acc_scr[c] = alpha * acc_scr[c] + jnp.dot(vtb, p.astype(BF16), preferred_element_type=F32)

    scores(st_a, 0)

    def pair(jp, carry):
        j = 2 * jp
        scores(st_b, j + 1)
        accumulate(st_a, j, False)
        scores(st_a, j + 2)
        accumulate(st_b, j + 1, False)
        return carry

    lax.fori_loop(0, qi // 2, pair, 0)

    @pl.when(qi % 2 == 1)
    def _():
        scores(st_b, qi)
        accumulate(st_a, qi - 1, False)
        accumulate(st_b, qi, True)

    @pl.when(qi % 2 == 0)
    def _():
        accumulate(st_a, qi, True)

    lp = lam_ref[...]
    lam = (jnp.exp(jnp.sum(lp[0:1] * lp[1:2], axis=1, keepdims=True))
           - jnp.exp(jnp.sum(lp[2:3] * lp[3:4], axis=1, keepdims=True)) + lam_init)
    ot = acc_scr[0] / l_scr[0:1, :] - lam * (acc_scr[1] / l_scr[1:2, :])
    ot = ot * lax.rsqrt(jnp.mean(ot * ot, axis=0, keepdims=True) + EPS) * sg_ref[...] * (1.0 - lam_init)
    o_ref[0] = ot.T.astype(o_ref.dtype)


def _attention(q, k, vt, lam_params, subln_g, lam_init):
    b, s, nq = q.shape
    heads = nq // (2 * HEAD_DIM)
    tq = min(TILE_Q, s)
    return pl.pallas_call(
        functools.partial(_attn_kernel, lam_init=lam_init),
        out_shape=jax.ShapeDtypeStruct((b, s, heads * V_DIM), BF16),
        grid=(b, heads, s // tq),
        in_specs=[pl.BlockSpec((1, tq, 2 * HEAD_DIM), lambda i, h, j: (i, j, h)),
                  pl.BlockSpec((1, s, 2 * HEAD_DIM), lambda i, h, j: (i, 0, h)),
                  pl.BlockSpec((1, V_DIM, s), lambda i, h, j: (i, h, 0)),
                  pl.BlockSpec(lam_params.shape, lambda i, h, j: (0, 0)),
                  pl.BlockSpec((V_DIM, 1), lambda i, h, j: (0, 0))],
        out_specs=pl.BlockSpec((1, tq, V_DIM), lambda i, h, j: (i, j, h)),
        scratch_shapes=[pltpu.VMEM((2, tq), F32), pltpu.VMEM((2, tq), F32), pltpu.VMEM((2, V_DIM, tq), F32),
                        pltpu.VMEM((2, tq, tq), F32), pltpu.VMEM((2, tq, tq), F32)],
        compiler_params=_params("arbitrary", "arbitrary", "arbitrary"), name="diff_attention",
    )(q, k, vt, lam_params, subln_g.reshape(V_DIM, 1))


def kernel(x, p, positions, norm_mix, norm_ffn, conv_w_pw1, conv_b_pw1, conv_w_dw, conv_b_dw, conv_ln_g, conv_ln_b,
           conv_w_pw2, conv_b_pw2, da_w_qkv, da_lambda, da_subln, da_w_o, moe_w_rg, moe_b_rg, moe_w_re, moe_b_re,
           moe_w_gate, moe_w_up, moe_w_down, ple_norm, ple_w_gate, ple_w_proj, final_norm):
    b, s, d = x.shape
    n = b * s
    depth = norm_mix.shape[0]
    assert depth == 2, "layer 0 is the conv mixer, layer 1 differential attention"
    pf = p.reshape(depth, n, -1)

    def moe(i, xin, final, attn_o=None, w_o=None):
        return _moe_ple(xin, pf[i], norm_ffn[i], moe_w_rg[i], moe_b_rg[i], moe_w_re[i], moe_b_re[i],
                        moe_w_gate[i], moe_w_up[i], moe_w_down[i], ple_norm[i], ple_w_gate[i], ple_w_proj[i],
                        final_norm, final, attn_o, w_o)

    x = _conv_mixer(x, norm_mix[0], conv_w_pw1[0], conv_b_pw1[0], conv_w_dw[0], conv_b_dw[0],
                    conv_ln_g[0], conv_ln_b[0], conv_w_pw2[0], conv_b_pw2[0])
    x = moe(0, x.reshape(n, d), False)

    lam_init = 0.8 - 0.6 * math.exp(-0.3 * 1)
    q, k, vt = _qkv(x.reshape(b, s, d), positions, norm_mix[1], da_w_qkv[0])
    o = _attention(q, k, vt, da_lambda[0], da_subln[0], lam_init)
    x = moe(1, x, True, o.reshape(n, -1), da_w_o[0])
    return x.reshape(b, s, d)
```

```python
import functools
import math

import numpy as np
import jax
import jax.numpy as jnp
from jax import lax
from jax.experimental import pallas as pl
from jax.experimental.pallas import tpu as pltpu

F32 = jnp.float32
BF16 = jnp.bfloat16
I32 = jnp.int32

EPS = 1e-6
CONV_WIDTH = 31
CONV_HALO = 32
HEAD_DIM = 64
V_DIM = 128
ROPE_DIM = 16
ROPE_THETA = 500000.0
N_GROUPS = 4
N_EXP = 8
N_PAIRS = N_EXP * (N_EXP - 1) // 2
N_CLASSES = N_GROUPS * N_PAIRS
LANES = 128
SUBLANES = 8
VMEM_LIMIT = 56 * 1024 * 1024

TILE_CONV = 512
TILE_ROUTE = 512
TILE_DISP = 512
TILE_EXP = 256
TILE_PLE = 512
TILE_QKV = 512
TILE_Q = 512
TILE_K = 256
CONV_GROUP = 4
CONV_GAP = 4


def _pair_tables():
    remaining = list(range(N_EXP))
    order = []
    center, center_slot = 0, 0
    while len(remaining) > 1:
        others = [v for v in remaining if v != center]
        for v in others:
            order.append((center, v) if center_slot == 0 else (v, center))
        remaining.remove(center)
        center, center_slot = others[-1], 1 - center_slot
    assert len(order) == N_PAIRS and len({frozenset(p) for p in order}) == N_PAIRS
    cls = np.zeros((N_EXP * N_EXP,), np.float32)
    first_is_a = np.zeros((N_EXP * N_EXP,), np.float32)
    for idx, (a, b) in enumerate(order):
        cls[a * N_EXP + b] = idx
        cls[b * N_EXP + a] = idx
        first_is_a[a * N_EXP + b] = 1.0
    tbl = np.zeros((8, LANES), np.float32)
    tbl[0, : N_EXP * N_EXP] = cls
    tbl[1, : N_EXP * N_EXP] = first_is_a
    exp_a = np.array([g * N_EXP + a for g in range(N_GROUPS) for (a, _) in order], np.int32)
    exp_b = np.array([g * N_EXP + b for g in range(N_GROUPS) for (_, b) in order], np.int32)
    return tbl, exp_a, exp_b


_PAIR_TBL, _CLASS_EXP_A, _CLASS_EXP_B = _pair_tables()


def _rms(x, g):
    return x * lax.rsqrt(jnp.mean(x * x, axis=-1, keepdims=True) + EPS) * g


def _params(*sem):
    return pltpu.CompilerParams(dimension_semantics=sem, vmem_limit_bytes=VMEM_LIMIT)


def _conv_kernel(x_ref, g_ref, w1_ref, b1_ref, w3_ref, bdw_ref, lng_ref, lnb_ref, w2_ref, b2_ref,
                 o_ref, vbuf, cbuf):
    nslab, t, _ = cbuf.shape
    c = nslab * LANES
    seg = t // SUBLANES
    pitch = CONV_HALO + seg + CONV_GAP
    x = x_ref[0]
    h = _rms(x, g_ref[...])
    u = jnp.dot(h.astype(BF16), w1_ref[...], preferred_element_type=F32) + b1_ref[...]
    v = u[:, :c] * jax.nn.sigmoid(u[:, c:])

    last = (SUBLANES - 1) * pitch + seg
    @pl.when(pl.program_id(1) == 0)
    def _():
        vbuf[:, 0:CONV_HALO, :] = jnp.zeros((nslab, CONV_HALO, LANES), F32)

    @pl.when(pl.program_id(1) > 0)
    def _():
        vbuf[:, 0:CONV_HALO, :] = vbuf[:, last:last + CONV_HALO, :]

    for s in range(nslab):
        vs = v[:, s * LANES:(s + 1) * LANES]
        for j in range(SUBLANES):
            if j > 0:
                vbuf[s, j * pitch:j * pitch + CONV_HALO, :] = vs[j * seg - CONV_HALO:j * seg]
            vbuf[s, j * pitch + CONV_HALO:j * pitch + CONV_HALO + seg, :] = vs[j * seg:(j + 1) * seg]

    first = CONV_HALO - (CONV_WIDTH - 1)
    for s in range(nslab):
        wv = [jnp.broadcast_to(w3_ref[s, k:k + 1, :], (SUBLANES, LANES)) for k in range(CONV_WIDTH)]
        bias = jnp.broadcast_to(bdw_ref[:, s * LANES:(s + 1) * LANES], (SUBLANES, LANES))

        def step(g, carry, s=s, wv=wv, bias=bias):
            i0 = g * CONV_GROUP
            accs = [[bias, None] for _ in range(CONV_GROUP)]
            for off in range(CONV_GROUP + CONV_WIDTH - 1):
                val = vbuf[s, pl.ds(i0 + first + off, SUBLANES, stride=pitch), :]
                for u in range(CONV_GROUP):
                    k = off - u
                    if 0 <= k < CONV_WIDTH:
                        prod = wv[k] * val
                        accs[u][k % 2] = prod if accs[u][k % 2] is None else accs[u][k % 2] + prod
            for u in range(CONV_GROUP):
                cbuf[s, pl.ds(pl.multiple_of((i0 + u) * SUBLANES, SUBLANES), SUBLANES), :] = accs[u][0] + accs[u][1]
            return carry

        lax.fori_loop(0, seg // CONV_GROUP, step, 0)

    y = jnp.concatenate(
        [jnp.concatenate([cbuf[s, pl.ds(j, seg, stride=SUBLANES), :] for j in range(SUBLANES)], axis=0)
         for s in range(nslab)], axis=1)
    mu = jnp.mean(y, axis=-1, keepdims=True)
    d = y - mu
    var = jnp.mean(d * d, axis=-1, keepdims=True)
    y = d * lax.rsqrt(var + EPS) * lng_ref[...] + lnb_ref[...]
    y = y * jax.nn.sigmoid(y)
    o_ref[0] = x + jnp.dot(y.astype(BF16), w2_ref[...], preferred_element_type=F32) + b2_ref[...]


def _conv_mixer(x, g, w1, b1, wdw, bdw, lng, lnb, w2, b2):
    b, s, d = x.shape
    c = w2.shape[0]
    t = min(TILE_CONV, s)
    seg = t // SUBLANES
    assert seg >= CONV_HALO and seg % SUBLANES == 0 and c % LANES == 0
    nslab = c // LANES
    pitch = CONV_HALO + seg + CONV_GAP
    row = lambda a: a.reshape(1, -1)
    full = lambda a: pl.BlockSpec(a.shape, lambda i, j: (0,) * a.ndim)
    w3 = jnp.pad(wdw, ((0, CONV_HALO - CONV_WIDTH), (0, 0))).reshape(CONV_HALO, nslab, LANES).transpose(1, 0, 2)
    args = (row(g), w1.astype(BF16), row(b1), w3, row(bdw), row(lng), row(lnb), w2.astype(BF16), row(b2))
    return pl.pallas_call(
        _conv_kernel,
        out_shape=jax.ShapeDtypeStruct(x.shape, F32),
        grid=(b, s // t),
        in_specs=[pl.BlockSpec((1, t, d), lambda i, j: (i, j, 0))] + [full(a) for a in args],
        out_specs=pl.BlockSpec((1, t, d), lambda i, j: (i, j, 0)),
        scratch_shapes=[pltpu.VMEM((nslab, SUBLANES * pitch, LANES), F32), pltpu.VMEM((nslab, t, LANES), F32)],
        compiler_params=_params("arbitrary", "arbitrary"),
        name="conv_mixer",
    )(x, *args)


def _route_body(x, g_ref, wr_ref, br_ref, tri_ref, tbl_ref, h3_ref, wts_ref, meta_ref, cnt_ref, cnt_scr):
    t, d = x.shape

    @pl.when(pl.program_id(0) == 0)
    def _():
        cnt_scr[...] = jnp.zeros(cnt_scr.shape, F32)

    h = _rms(x, g_ref[...])
    lg = jnp.dot(h.astype(BF16), wr_ref[...], preferred_element_type=F32) + br_ref[...]
    lane = lax.broadcasted_iota(I32, (t, LANES), 1)
    ninf = -jnp.inf

    gmask = lane < N_GROUPS
    gl = jnp.where(gmask, lg, ninf)
    gmax = jnp.max(gl, axis=1, keepdims=True)
    gidx = jnp.min(jnp.where(gl == gmax, lane, LANES), axis=1, keepdims=True)
    gsum = jnp.sum(jnp.where(gmask, jnp.exp(lg - gmax), 0.0), axis=1, keepdims=True)
    gw = 1.0 / gsum

    lo = N_GROUPS + gidx * N_EXP
    el = jnp.where(lane >= lo, jnp.where(lane < lo + N_EXP, lg, ninf), ninf)
    m1 = jnp.max(el, axis=1, keepdims=True)
    i1 = jnp.min(jnp.where(el == m1, lane, LANES), axis=1, keepdims=True)
    el2 = jnp.where(lane == i1, ninf, el)
    m2 = jnp.max(el2, axis=1, keepdims=True)
    i2 = jnp.min(jnp.where(el2 == m2, lane, LANES), axis=1, keepdims=True)
    dlt = jnp.exp(m2 - m1)
    p1 = 1.0 / (1.0 + dlt)
    w1 = p1 * gw
    w2 = dlt * p1 * gw

    q = (i1 - lo) * N_EXP + (i2 - lo)
    qoh = lane == q
    cl = jnp.sum(jnp.where(qoh, tbl_ref[0:1, :], 0.0), axis=1, keepdims=True)
    fa = jnp.sum(jnp.where(qoh, tbl_ref[1:2, :], 0.0), axis=1, keepdims=True)
    cls = gidx * N_PAIRS + cl.astype(I32)
    first_a = fa > 0.5
    wa = jnp.where(first_a, w1, w2)
    wb = jnp.where(first_a, w2, w1)

    oh = lane == cls
    cum = jnp.dot(tri_ref[...], jnp.where(oh, 1.0, 0.0).astype(BF16), preferred_element_type=F32)
    cnt = cnt_scr[...]
    rank = jnp.sum(jnp.where(oh, cum + cnt, 0.0), axis=1, keepdims=True) - 1.0
    cnt = cnt + cum[t - 1:t, :]
    cnt_scr[...] = cnt
    cnt_ref[...] = cnt

    for s in range(d // LANES):
        h3_ref[pl.ds(s, t, stride=SUBLANES), :] = h[:, s * LANES:(s + 1) * LANES]
    lane8 = lax.broadcasted_iota(I32, (t, 8), 1)
    wts_ref[...] = jnp.where(lane8 == 0, wa, jnp.where(lane8 == 1, wb, 0.0))
    meta_ref[...] = jnp.where(lane8 == 0, cls, jnp.where(lane8 == 1, rank.astype(I32), 0))


def _route0_kernel(x_ref, g_ref, wr_ref, br_ref, tri_ref, tbl_ref, h3_ref, wts_ref, meta_ref, cnt_ref, cnt_scr):
    _route_body(x_ref[...], g_ref, wr_ref, br_ref, tri_ref, tbl_ref, h3_ref, wts_ref, meta_ref, cnt_ref, cnt_scr)


def _route1_kernel(x_ref, o_ref, wo_ref, g_ref, wr_ref, br_ref, tri_ref, tbl_ref,
                   xo_ref, h3_ref, wts_ref, meta_ref, cnt_ref, cnt_scr):
    x = x_ref[...] + jnp.dot(o_ref[...], wo_ref[...], preferred_element_type=F32)
    xo_ref[...] = x
    _route_body(x, g_ref, wr_ref, br_ref, tri_ref, tbl_ref, h3_ref, wts_ref, meta_ref, cnt_ref, cnt_scr)


def _route(x, g, w_rg, b_rg, w_re, b_re, attn_o=None, w_o=None):
    n, d = x.shape
    assert d == SUBLANES * LANES, "one (8, 128) tile per token row"
    t = min(TILE_ROUTE, n)
    wr = jnp.zeros((d, LANES), F32)
    wr = wr.at[:, :N_GROUPS].set(w_rg)
    wr = wr.at[:, N_GROUPS:N_GROUPS + N_GROUPS * N_EXP].set(jnp.transpose(w_re, (1, 0, 2)).reshape(d, -1))
    br = jnp.zeros((1, LANES), F32)
    br = br.at[0, :N_GROUPS].set(b_rg)
    br = br.at[0, N_GROUPS:N_GROUPS + N_GROUPS * N_EXP].set(b_re.reshape(-1))
    tri = jnp.asarray(np.tril(np.ones((t, t), np.float32)), BF16)
    tbl = jnp.asarray(_PAIR_TBL)
    full = lambda a: pl.BlockSpec(a.shape, lambda i: (0,) * a.ndim)
    rows = lambda w: pl.BlockSpec((t, w), lambda i: (i, 0))
    common = (g.reshape(1, -1), wr.astype(BF16), br, tri, tbl)
    out_shape = [jax.ShapeDtypeStruct((n * SUBLANES, LANES), F32), jax.ShapeDtypeStruct((n, 8), F32),
                 jax.ShapeDtypeStruct((n, 8), I32), jax.ShapeDtypeStruct((1, LANES), F32)]
    out_specs = [pl.BlockSpec((t * SUBLANES, LANES), lambda i: (i, 0)), rows(8), rows(8),
                 pl.BlockSpec((1, LANES), lambda i: (0, 0))]
    if attn_o is None:
        h3, wts, meta, cnt = pl.pallas_call(
            _route0_kernel, out_shape=out_shape, grid=(n // t,),
            in_specs=[rows(d)] + [full(a) for a in common], out_specs=out_specs,
            scratch_shapes=[pltpu.VMEM((1, LANES), F32)],
            compiler_params=_params("arbitrary"), name="moe_route",
        )(x, *common)
        return x, h3, wts, meta, cnt
    wo = w_o.astype(BF16)
    xo, h3, wts, meta, cnt = pl.pallas_call(
        _route1_kernel, out_shape=[jax.ShapeDtypeStruct((n, d), F32)] + out_shape, grid=(n // t,),
        in_specs=[rows(d), rows(attn_o.shape[1]), full(wo)] + [full(a) for a in common],
        out_specs=[rows(d)] + out_specs,
        scratch_shapes=[pltpu.VMEM((1, LANES), F32)],
        compiler_params=_params("arbitrary"), name="attn_out_moe_route",
    )(x, attn_o, wo, *common)
    return xo, h3, wts, meta, cnt


def _plan(meta, cnt, n):
    tm = TILE_EXP
    nt_max = n // tm + N_CLASSES
    cnt = cnt[0, :N_CLASSES].astype(I32)
    nt = (cnt + tm - 1) // tm
    tend = jnp.cumsum(nt)
    offs = (tend - nt) * tm
    total = tend[-1]
    classes = jnp.arange(N_CLASSES, dtype=I32)
    dest = jnp.sum(jnp.where(meta[:, 0:1] == classes[None, :], offs[None, :], 0), axis=1) + meta[:, 1]
    tile = jnp.arange(nt_max, dtype=I32)
    rowblk = jnp.minimum(tile, total - 1)
    tcls = jnp.sum((tend[None, :] <= rowblk[:, None]).astype(I32), axis=1)
    active = (tile < total).astype(I32)
    exp_a = jnp.take(jnp.asarray(_CLASS_EXP_A), tcls)
    exp_b = jnp.take(jnp.asarray(_CLASS_EXP_B), tcls)
    pad = jnp.zeros((LANES,), I32)
    padstart = pad.at[:N_CLASSES].set(offs + cnt)
    padstart = padstart.at[N_CLASSES].set(total * 2)
    padlen = pad.at[:N_CLASSES].set(nt * tm - cnt)
    return dest, rowblk, active, exp_a, exp_b, padstart, padlen, nt_max


_PAD_BITS = tuple(1 << b for b in reversed(range(int(math.log2(TILE_EXP)))))
DMA_UNROLL = 8


def _dispatch_kernel(dest_ref, padstart_ref, padlen_ref, h3_ref, hs_ref, zbuf, sem):
    rows = h3_ref.shape[0]
    t = rows // SUBLANES
    i = pl.program_id(0)

    def issue(g, carry):
        for u in range(DMA_UNROLL):
            r = g * DMA_UNROLL + u
            d = dest_ref[i * t + r]
            pltpu.make_async_copy(h3_ref.at[pl.ds(pl.multiple_of(r * SUBLANES, SUBLANES), SUBLANES)],
                                  hs_ref.at[pl.ds(pl.multiple_of(d * SUBLANES, SUBLANES), SUBLANES)],
                                  sem.at[0]).start(priority=u % 2)
        return carry

    lax.fori_loop(0, t // DMA_UNROLL, issue, 0)
    pltpu.make_async_copy(h3_ref, hs_ref.at[pl.ds(0, rows)], sem.at[0]).wait()

    @pl.when(i == pl.num_programs(0) - 1)
    def _():
        zbuf[...] = jnp.zeros(zbuf.shape, F32)

        def pad_copy(start, size):
            return pltpu.make_async_copy(zbuf.at[pl.ds(0, size * SUBLANES)],
                                         hs_ref.at[pl.ds(pl.multiple_of(start * SUBLANES, SUBLANES), size * SUBLANES)],
                                         sem.at[1])

        def pieces(c, fn):
            start = padstart_ref[c]
            left = padlen_ref[c]
            for bit in _PAD_BITS:
                has = (left & bit) != 0

                @pl.when(has)
                def _(start=start, bit=bit):
                    fn(pad_copy(start, bit))

                start = start + jnp.where(has, bit, 0)

        def issue_pad(c, carry):
            pieces(c, lambda cp: cp.start())
            return carry

        def drain_pad(c, carry):
            pieces(c, lambda cp: cp.wait())
            return carry

        lax.fori_loop(0, N_CLASSES, issue_pad, 0)
        lax.fori_loop(0, N_CLASSES, drain_pad, 0)

        half = zbuf.shape[0] // SUBLANES
        first_unused = padstart_ref[N_CLASSES]
        n_halves = hs_ref.shape[0] // zbuf.shape[0]

        def issue_tail(j, carry):
            pad_copy(j * half, half).start()
            return carry

        def drain_tail(j, carry):
            pad_copy(j * half, half).wait()
            return carry

        lax.fori_loop(first_unused, n_halves, issue_tail, 0)
        lax.fori_loop(first_unused, n_halves, drain_tail, 0)


def _dispatch(h3, dest, padstart, padlen, nt_max):
    n = h3.shape[0] // SUBLANES
    t = min(TILE_DISP, n)
    return pl.pallas_call(
        _dispatch_kernel,
        out_shape=jax.ShapeDtypeStruct((nt_max * TILE_EXP * SUBLANES, LANES), F32),
        grid_spec=pltpu.PrefetchScalarGridSpec(
            num_scalar_prefetch=3, grid=(n // t,),
            in_specs=[pl.BlockSpec((t * SUBLANES, LANES), lambda i, *_: (i, 0))],
            out_specs=pl.BlockSpec(memory_space=pl.ANY),
            scratch_shapes=[pltpu.VMEM((TILE_EXP // 2 * SUBLANES, LANES), F32), pltpu.SemaphoreType.DMA((2,))]),
        compiler_params=_params("arbitrary"), name="moe_dispatch",
    )(dest, padstart, padlen, h3)


def _expert_kernel(rowblk_ref, active_ref, ea_ref, eb_ref, hs_ref,
                   wga_ref, wua_ref, wda_ref, wgb_ref, wub_ref, wdb_ref, ys_ref, rec_scr):
    tm = hs_ref.shape[0] // SUBLANES
    i = pl.program_id(0)

    @pl.when(active_ref[i] == 1)
    def _():
        x = jnp.concatenate([hs_ref[pl.ds(s, tm, stride=SUBLANES), :] for s in range(SUBLANES)], axis=1).astype(BF16)
        for slot, (wg, wu, wd) in enumerate(((wga_ref, wua_ref, wda_ref), (wgb_ref, wub_ref, wdb_ref))):
            hg = jnp.dot(x, wg[0], preferred_element_type=F32)
            ug = jnp.dot(x, wu[0], preferred_element_type=F32)
            act = hg * jax.nn.sigmoid(hg) * ug
            y = jnp.dot(act.astype(BF16), wd[0], preferred_element_type=F32)
            for s in range(SUBLANES):
                rec_scr[slot, pl.ds(s, tm, stride=SUBLANES), :] = y[:, s * LANES:(s + 1) * LANES]
        rec = jnp.stack([rec_scr[0].reshape(tm, SUBLANES, LANES), rec_scr[1].reshape(tm, SUBLANES, LANES)], axis=1)
        ys_ref[...] = rec.reshape(tm * 2 * SUBLANES, LANES).astype(ys_ref.dtype)

    @pl.when(active_ref[i] == 0)
    def _():
        ys_ref[...] = jnp.zeros(ys_ref.shape, ys_ref.dtype)


def _experts(hs, rowblk, active, exp_a, exp_b, w_gate, w_up, w_down):
    tm = TILE_EXP
    rows = hs.shape[0] // SUBLANES
    d = SUBLANES * LANES
    f = w_gate.shape[-1]
    wg = w_gate.reshape(-1, d, f).astype(BF16)
    wu = w_up.reshape(-1, d, f).astype(BF16)
    wd = w_down.reshape(-1, f, d).astype(BF16)
    in_a = lambda shp: pl.BlockSpec((1,) + shp, lambda i, rb, ac, ea, eb: (ea[i], 0, 0))
    in_b = lambda shp: pl.BlockSpec((1,) + shp, lambda i, rb, ac, ea, eb: (eb[i], 0, 0))
    return pl.pallas_call(
        _expert_kernel,
        out_shape=jax.ShapeDtypeStruct((rows * 2 * SUBLANES, LANES), BF16),
        grid_spec=pltpu.PrefetchScalarGridSpec(
            num_scalar_prefetch=4, grid=(rows // tm,),
            in_specs=[pl.BlockSpec((tm * SUBLANES, LANES), lambda i, rb, ac, ea, eb: (rb[i], 0)),
                      in_a((d, f)), in_a((d, f)), in_a((f, d)), in_b((d, f)), in_b((d, f)), in_b((f, d))],
            out_specs=pl.BlockSpec((tm * 2 * SUBLANES, LANES), lambda i, *_: (i, 0)),
            scratch_shapes=[pltpu.VMEM((2, tm * SUBLANES, LANES), F32)]),
        compiler_params=_params("arbitrary"), name="moe_experts",
    )(rowblk, active, exp_a, exp_b, hs, wg, wu, wd, wg, wu, wd)


def _ple_kernel(dest_ref, x_ref, wts_ref, p_ref, g_ref, wg_ref, wp_ref, fg_ref, ys_ref, o_ref, ybuf0, ybuf1, rec_scr,
                sem, *, final):
    t = x_ref.shape[0]
    rec = 2 * SUBLANES
    i = pl.program_id(0)
    last = pl.num_programs(0) - 1

    def record_copy(d, row, buf, s):
        return pltpu.make_async_copy(ys_ref.at[pl.ds(pl.multiple_of(d * rec, rec), rec)],
                                     buf.at[pl.ds(row, rec)], sem.at[s])

    def tile_wait(buf, s):
        pltpu.make_async_copy(ys_ref.at[pl.ds(0, t * rec)], buf, sem.at[s]).wait()

    @pl.when(i == 0)
    def _():
        def issue(r, carry):
            record_copy(dest_ref[r], pl.multiple_of(r * rec, rec), ybuf0, 0).start()
            return carry
        lax.fori_loop(0, t, issue, 0, unroll=DMA_UNROLL)

    def step(cur, cs, nxt, ns):
        tile_wait(cur, cs)
        base = jnp.minimum(i + 1, last) * t
        for r in range(t):
            record_copy(dest_ref[base + r], r * rec, nxt, ns).start()

        recs = cur[...].astype(F32).reshape(t, 2, SUBLANES, LANES)
        wts = wts_ref[...]
        x = x_ref[...]
        for e in range(2):
            rec_scr[e] = recs[:, e].reshape(t * SUBLANES, LANES)
            ye = jnp.concatenate([rec_scr[e, pl.ds(s, t, stride=SUBLANES), :] for s in range(SUBLANES)], axis=1)
            x = x + wts[:, e:e + 1] * ye
        gate = jax.nn.sigmoid(jnp.dot(_rms(x, g_ref[...]).astype(BF16), wg_ref[...], preferred_element_type=F32))
        x = x + gate * jnp.dot(p_ref[0].astype(BF16), wp_ref[...], preferred_element_type=F32)
        if final:
            x = _rms(x, fg_ref[...])
        o_ref[...] = x

        @pl.when(i == last)
        def _():
            tile_wait(nxt, ns)

    @pl.when(i % 2 == 0)
    def _():
        step(ybuf0, 0, ybuf1, 1)

    @pl.when(i % 2 == 1)
    def _():
        step(ybuf1, 1, ybuf0, 0)


def _combine_ple(x, ys, dest, wts, p, layer, g, w_gate, w_proj, final_g, final):
    n, d = x.shape
    t = min(TILE_PLE, n)
    rec = 2 * SUBLANES
    full = lambda a: pl.BlockSpec(a.shape, lambda i, *_: (0,) * a.ndim)
    rows = lambda w: pl.BlockSpec((t, w), lambda i, *_: (i, 0))
    args = (g.reshape(1, -1), w_gate.astype(BF16), w_proj.astype(BF16), final_g.reshape(1, -1))
    return pl.pallas_call(
        functools.partial(_ple_kernel, final=final),
        out_shape=jax.ShapeDtypeStruct((n, d), F32),
        grid_spec=pltpu.PrefetchScalarGridSpec(
            num_scalar_prefetch=1, grid=(n // t,),
            in_specs=[rows(d), rows(wts.shape[1]), pl.BlockSpec((1, t, p.shape[2]), lambda i, *_: (layer, i, 0))]
                     + [full(a) for a in args] + [pl.BlockSpec(memory_space=pl.ANY)],
            out_specs=rows(d),
            scratch_shapes=[pltpu.VMEM((t * rec, LANES), ys.dtype), pltpu.VMEM((t * rec, LANES), ys.dtype),
                            pltpu.VMEM((2, t * SUBLANES, LANES), F32), pltpu.SemaphoreType.DMA((2,))]),
        compiler_params=_params("arbitrary"), name="moe_combine_ple",
    )(dest, x, wts, p, *args, ys)


def _moe_ple(x, p, layer, g_ffn, w_rg, b_rg, w_re, b_re, w_gate, w_up, w_down, ple_g, ple_wg, ple_wp, final_g, final,
             attn_o=None, w_o=None):
    n = x.shape[0]
    x, h3, wts, meta, cnt = _route(x, g_ffn, w_rg, b_rg, w_re, b_re, attn_o, w_o)
    dest, rowblk, active, exp_a, exp_b, padstart, padlen, nt_max = _plan(meta, cnt, n)
    hs = _dispatch(h3, dest, padstart, padlen, nt_max)
    ys = _experts(hs, rowblk, active, exp_a, exp_b, w_gate, w_up, w_down)
    return _combine_ple(x, ys, dest, wts, p, layer, ple_g, ple_wg, ple_wp, final_g, final)


def _qkv_kernel(x_ref, pos_ref, g_ref, wq_ref, wk_ref, wv_ref, invf_ref, q_ref, k_ref, vt_ref, *, qscale):
    t = x_ref.shape[1]
    h = _rms(x_ref[0], g_ref[...]).astype(BF16)
    lane = lax.broadcasted_iota(I32, (t, LANES), 1) % HEAD_DIM
    ang = pos_ref[0].astype(F32) * invf_ref[...]
    cos = jnp.cos(ang)
    sin = jnp.sin(ang)
    half = ROPE_DIM // 2
    cmul = jnp.where(lane < ROPE_DIM, cos, 1.0)
    s_up = jnp.where(lane < half, -sin, 0.0)
    s_dn = jnp.where(lane >= half, jnp.where(lane < ROPE_DIM, sin, 0.0), 0.0)

    def rope(w_ref, o_ref, scale):
        y = jnp.dot(h, w_ref[...], preferred_element_type=F32)
        for c0 in range(0, y.shape[1], LANES):
            yb = y[:, c0:c0 + LANES]
            rot = yb * cmul + pltpu.roll(yb, LANES - half, 1) * s_up + pltpu.roll(yb, half, 1) * s_dn
            o_ref[0, :, c0:c0 + LANES] = (rot * scale).astype(o_ref.dtype)

    rope(wq_ref, q_ref, qscale)
    rope(wk_ref, k_ref, 1.0)
    v = jnp.dot(h, wv_ref[...], preferred_element_type=F32)
    vt_ref[0] = v.T.astype(vt_ref.dtype)


def _qkv(x, positions, g, w_qkv):
    b, s, d = x.shape
    t = min(TILE_QKV, s)
    nq = (w_qkv.shape[1] - d) // 2
    w = w_qkv.astype(BF16)
    wq, wk, wv = w[:, :nq], w[:, nq:2 * nq], w[:, 2 * nq:]
    inv_freq = ROPE_THETA ** (-jnp.arange(0, ROPE_DIM, 2, dtype=F32) / ROPE_DIM)
    lane = np.arange(LANES) % HEAD_DIM
    invf = jnp.where(lane < ROPE_DIM, jnp.take(inv_freq, lane % (ROPE_DIM // 2)), 0.0).reshape(1, LANES)
    qscale = HEAD_DIM ** -0.5 * math.log2(math.e)
    full = lambda a: pl.BlockSpec(a.shape, lambda i, j: (0,) * a.ndim)
    args = (g.reshape(1, -1), wq, wk, wv, invf)
    return pl.pallas_call(
        functools.partial(_qkv_kernel, qscale=qscale),
        out_shape=[jax.ShapeDtypeStruct((b, s, nq), BF16), jax.ShapeDtypeStruct((b, s, nq), BF16),
                   jax.ShapeDtypeStruct((b, d, s), BF16)],
        grid=(b, s // t),
        in_specs=[pl.BlockSpec((1, t, d), lambda i, j: (i, j, 0)), pl.BlockSpec((1, t, 1), lambda i, j: (i, j, 0))]
                 + [full(a) for a in args],
        out_specs=[pl.BlockSpec((1, t, nq), lambda i, j: (i, j, 0)), pl.BlockSpec((1, t, nq), lambda i, j: (i, j, 0)),
                   pl.BlockSpec((1, d, t), lambda i, j: (i, 0, j))],
        compiler_params=_params("arbitrary", "arbitrary"), name="qkv_rope",
    )(x, positions.reshape(b, s, 1), *args)


def _attn_kernel(q_ref, k_ref, vt_ref, lam_ref, sg_ref, o_ref, m_scr, l_scr, acc_scr, st_a, st_b, *, lam_init):
    tq = q_ref.shape[1]
    tk = tq
    qi = pl.program_id(2)
    q = q_ref[0]
    qs = [q[:, c * HEAD_DIM:(c + 1) * HEAD_DIM] for c in range(2)]
    m_scr[...] = jnp.full(m_scr.shape, -1e30, F32)
    l_scr[...] = jnp.zeros(l_scr.shape, F32)
    acc_scr[...] = jnp.zeros(acc_scr.shape, F32)

    def scores(buf, j):
        kb = k_ref[0, pl.ds(pl.multiple_of(j * tk, tk), tk), :]
        for c in range(2):
            buf[c] = lax.dot_general(kb[:, c * HEAD_DIM:(c + 1) * HEAD_DIM], qs[c], (((1,), (1,)), ((), ())),
                                     preferred_element_type=F32)

    def accumulate(buf, j, masked):
        vtb = vt_ref[0, :, pl.ds(pl.multiple_of(j * tk, tk), tk)]
        if masked:
            keep = lax.broadcasted_iota(I32, (tk, tq), 1) >= lax.broadcasted_iota(I32, (tk, tq), 0)
        for c in range(2):
            st = buf[c]
            if masked:
                st = jnp.where(keep, st, -1e30)
            m_old = m_scr[c:c + 1, :]
            m_new = jnp.maximum(m_old, jnp.max(st, axis=0, keepdims=True))
            alpha = jnp.exp2(m_old - m_new)
            p = jnp.exp2(st - m_new)
            l_scr[c:c + 1, :] = alpha * l_scr[c:c + 1, :] + jnp.sum(p, axis=0, keepdims=True)
            m_scr[c:c + 1, :] = m_new
            acc_scr[c] = alpha * acc_scr[c] + jnp.dot(vtb, p.astype(BF16), preferred_element_type=F32)

    scores(st_a, 0)

    def pair(jp, carry):
        j = 2 * jp
        scores(st_b, j + 1)
        accumulate(st_a, j, False)
        scores(st_a, j + 2)
        accumulate(st_b, j + 1, False)
        return carry

    lax.fori_loop(0, qi // 2, pair, 0)

    @pl.when(qi % 2 == 1)
    def _():
        scores(st_b, qi)
        accumulate(st_a, qi - 1, False)
        accumulate(st_b, qi, True)

    @pl.when(qi % 2 == 0)
    def _():
        accumulate(st_a, qi, True)

    lp = lam_ref[...]
    lam = (jnp.exp(jnp.sum(lp[0:1] * lp[1:2], axis=1, keepdims=True))
           - jnp.exp(jnp.sum(lp[2:3] * lp[3:4], axis=1, keepdims=True)) + lam_init)
    ot = acc_scr[0] / l_scr[0:1, :] - lam * (acc_scr[1] / l_scr[1:2, :])
    ot = ot * lax.rsqrt(jnp.mean(ot * ot, axis=0, keepdims=True) + EPS) * sg_ref[...] * (1.0 - lam_init)
    o_ref[0] = ot.T.astype(o_ref.dtype)


def _attention(q, k, vt, lam_params, subln_g, lam_init):
    b, s, nq = q.shape
    heads = nq // (2 * HEAD_DIM)
    tq = min(TILE_Q, s)
    return pl.pallas_call(
        functools.partial(_attn_kernel, lam_init=lam_init),
        out_shape=jax.ShapeDtypeStruct((b, s, heads * V_DIM), BF16),
        grid=(b, heads, s // tq),
        in_specs=[pl.BlockSpec((1, tq, 2 * HEAD_DIM), lambda i, h, j: (i, j, h)),
                  pl.BlockSpec((1, s, 2 * HEAD_DIM), lambda i, h, j: (i, 0, h)),
                  pl.BlockSpec((1, V_DIM, s), lambda i, h, j: (i, h, 0)),
                  pl.BlockSpec(lam_params.shape, lambda i, h, j: (0, 0)),
                  pl.BlockSpec((V_DIM, 1), lambda i, h, j: (0, 0))],
        out_specs=pl.BlockSpec((1, tq, V_DIM), lambda i, h, j: (i, j, h)),
        scratch_shapes=[pltpu.VMEM((2, tq), F32), pltpu.VMEM((2, tq), F32), pltpu.VMEM((2, V_DIM, tq), F32),
                        pltpu.VMEM((2, tq, tq), F32), pltpu.VMEM((2, tq, tq), F32)],
        compiler_params=_params("arbitrary", "arbitrary", "arbitrary"), name="diff_attention",
    )(q, k, vt, lam_params, subln_g.reshape(V_DIM, 1))


def kernel(x, p, positions, norm_mix, norm_ffn, conv_w_pw1, conv_b_pw1, conv_w_dw, conv_b_dw, conv_ln_g, conv_ln_b,
           conv_w_pw2, conv_b_pw2, da_w_qkv, da_lambda, da_subln, da_w_o, moe_w_rg, moe_b_rg, moe_w_re, moe_b_re,
           moe_w_gate, moe_w_up, moe_w_down, ple_norm, ple_w_gate, ple_w_proj, final_norm):
    b, s, d = x.shape
    n = b * s
    depth = norm_mix.shape[0]
    assert depth == 2, "layer 0 is the conv mixer, layer 1 differential attention"
    pf = p.reshape(depth, n, -1)

    def moe(i, xin, final, attn_o=None, w_o=None):
        return _moe_ple(xin, pf, i, norm_ffn[i], moe_w_rg[i], moe_b_rg[i], moe_w_re[i], moe_b_re[i],
                        moe_w_gate[i], moe_w_up[i], moe_w_down[i], ple_norm[i], ple_w_gate[i], ple_w_proj[i],
                        final_norm, final, attn_o, w_o)

    x = _conv_mixer(x, norm_mix[0], conv_w_pw1[0], conv_b_pw1[0], conv_w_dw[0], conv_b_dw[0],
                    conv_ln_g[0], conv_ln_b[0], conv_w_pw2[0], conv_b_pw2[0])
    x = moe(0, x.reshape(n, d), False)

    lam_init = 0.8 - 0.6 * math.exp(-0.3 * 1)
    q, k, vt = _qkv(x.reshape(b, s, d), positions, norm_mix[1], da_w_qkv[0])
    o = _attention(q, k, vt, da_lambda[0], da_subln[0], lam_init)
    x = moe(1, x, True, o.reshape(n, -1), da_w_o[0])
    return x.reshape(b, s, d)
```

```python
import functools
import math

import numpy as np
import jax
import jax.numpy as jnp
from jax import lax
from jax.experimental import pallas as pl
from jax.experimental.pallas import tpu as pltpu

F32 = jnp.float32
BF16 = jnp.bfloat16
I32 = jnp.int32

EPS = 1e-6
CONV_WIDTH = 31
CONV_HALO = 32
HEAD_DIM = 64
V_DIM = 128
ROPE_DIM = 16
ROPE_THETA = 500000.0
N_GROUPS = 4
N_EXP = 8
N_PAIRS = N_EXP * (N_EXP - 1) // 2
N_CLASSES = N_GROUPS * N_PAIRS
LANES = 128
SUBLANES = 8
VMEM_LIMIT = 56 * 1024 * 1024

TILE_CONV = 512
TILE_ROUTE = 512
TILE_DISP = 1024
TILE_EXP = 256
TILE_PLE = 512
TILE_QKV = 512
TILE_Q = 512
TILE_K = 256
CONV_GROUP = 8
CONV_GAP = 4


def _pair_tables():
    remaining = list(range(N_EXP))
    order = []
    center, center_slot = 0, 0
    while len(remaining) > 1:
        others = [v for v in remaining if v != center]
        for v in others:
            order.append((center, v) if center_slot == 0 else (v, center))
        remaining.remove(center)
        center, center_slot = others[-1], 1 - center_slot
    assert len(order) == N_PAIRS and len({frozenset(p) for p in order}) == N_PAIRS
    cls = np.zeros((N_EXP * N_EXP,), np.float32)
    first_is_a = np.zeros((N_EXP * N_EXP,), np.float32)
    for idx, (a, b) in enumerate(order):
        cls[a * N_EXP + b] = idx
        cls[b * N_EXP + a] = idx
        first_is_a[a * N_EXP + b] = 1.0
    tbl = np.zeros((8, LANES), np.float32)
    tbl[0, : N_EXP * N_EXP] = cls
    tbl[1, : N_EXP * N_EXP] = first_is_a
    exp_a = np.array([g * N_EXP + a for g in range(N_GROUPS) for (a, _) in order], np.int32)
    exp_b = np.array([g * N_EXP + b for g in range(N_GROUPS) for (_, b) in order], np.int32)
    return tbl, exp_a, exp_b


_PAIR_TBL, _CLASS_EXP_A, _CLASS_EXP_B = _pair_tables()


def _rms(x, g):
    return x * lax.rsqrt(jnp.mean(x * x, axis=-1, keepdims=True) + EPS) * g


def _params(*sem):
    return pltpu.CompilerParams(dimension_semantics=sem, vmem_limit_bytes=VMEM_LIMIT)


def _conv_kernel(x_ref, g_ref, w1_ref, b1_ref, w3_ref, bdw_ref, lng_ref, lnb_ref, w2_ref, b2_ref,
                 o_ref, vbuf, cbuf):
    nslab, t, _ = cbuf.shape
    c = nslab * LANES
    seg = t // SUBLANES
    pitch = CONV_HALO + seg + CONV_GAP
    x = x_ref[0]
    h = _rms(x, g_ref[...])
    u = jnp.dot(h.astype(BF16), w1_ref[...], preferred_element_type=F32) + b1_ref[...]
    v = u[:, :c] * jax.nn.sigmoid(u[:, c:])

    last = (SUBLANES - 1) * pitch + seg
    @pl.when(pl.program_id(1) == 0)
    def _():
        vbuf[:, 0:CONV_HALO, :] = jnp.zeros((nslab, CONV_HALO, LANES), F32)

    @pl.when(pl.program_id(1) > 0)
    def _():
        vbuf[:, 0:CONV_HALO, :] = vbuf[:, last:last + CONV_HALO, :]

    for s in range(nslab):
        vs = v[:, s * LANES:(s + 1) * LANES]
        for j in range(SUBLANES):
            if j > 0:
                vbuf[s, j * pitch:j * pitch + CONV_HALO, :] = vs[j * seg - CONV_HALO:j * seg]
            vbuf[s, j * pitch + CONV_HALO:j * pitch + CONV_HALO + seg, :] = vs[j * seg:(j + 1) * seg]

    first = CONV_HALO - (CONV_WIDTH - 1)
    for s in range(nslab):
        wv = [jnp.broadcast_to(w3_ref[s, k:k + 1, :], (SUBLANES, LANES)) for k in range(CONV_WIDTH)]
        bias = jnp.broadcast_to(bdw_ref[:, s * LANES:(s + 1) * LANES], (SUBLANES, LANES))

        def step(g, carry, s=s, wv=wv, bias=bias):
            i0 = g * CONV_GROUP
            accs = [[bias, None] for _ in range(CONV_GROUP)]
            for off in range(CONV_GROUP + CONV_WIDTH - 1):
                val = vbuf[s, pl.ds(i0 + first + off, SUBLANES, stride=pitch), :]
                for u in range(CONV_GROUP):
                    k = off - u
                    if 0 <= k < CONV_WIDTH:
                        prod = wv[k] * val
                        accs[u][k % 2] = prod if accs[u][k % 2] is None else accs[u][k % 2] + prod
            for u in range(CONV_GROUP):
                cbuf[s, pl.ds(pl.multiple_of((i0 + u) * SUBLANES, SUBLANES), SUBLANES), :] = accs[u][0] + accs[u][1]
            return carry

        lax.fori_loop(0, seg // CONV_GROUP, step, 0)

    y = jnp.concatenate(
        [jnp.concatenate([cbuf[s, pl.ds(j, seg, stride=SUBLANES), :] for j in range(SUBLANES)], axis=0)
         for s in range(nslab)], axis=1)
    mu = jnp.mean(y, axis=-1, keepdims=True)
    d = y - mu
    var = jnp.mean(d * d, axis=-1, keepdims=True)
    y = d * lax.rsqrt(var + EPS) * lng_ref[...] + lnb_ref[...]
    y = y * jax.nn.sigmoid(y)
    o_ref[0] = x + jnp.dot(y.astype(BF16), w2_ref[...], preferred_element_type=F32) + b2_ref[...]


def _conv_mixer(x, g, w1, b1, wdw, bdw, lng, lnb, w2, b2):
    b, s, d = x.shape
    c = w2.shape[0]
    t = min(TILE_CONV, s)
    seg = t // SUBLANES
    assert seg >= CONV_HALO and seg % SUBLANES == 0 and c % LANES == 0
    nslab = c // LANES
    pitch = CONV_HALO + seg + CONV_GAP
    row = lambda a: a.reshape(1, -1)
    full = lambda a: pl.BlockSpec(a.shape, lambda i, j: (0,) * a.ndim)
    w3 = jnp.pad(wdw, ((0, CONV_HALO - CONV_WIDTH), (0, 0))).reshape(CONV_HALO, nslab, LANES).transpose(1, 0, 2)
    args = (row(g), w1.astype(BF16), row(b1), w3, row(bdw), row(lng), row(lnb), w2.astype(BF16), row(b2))
    return pl.pallas_call(
        _conv_kernel,
        out_shape=jax.ShapeDtypeStruct(x.shape, F32),
        grid=(b, s // t),
        in_specs=[pl.BlockSpec((1, t, d), lambda i, j: (i, j, 0))] + [full(a) for a in args],
        out_specs=pl.BlockSpec((1, t, d), lambda i, j: (i, j, 0)),
        scratch_shapes=[pltpu.VMEM((nslab, SUBLANES * pitch, LANES), F32), pltpu.VMEM((nslab, t, LANES), F32)],
        compiler_params=_params("arbitrary", "arbitrary"),
        name="conv_mixer",
    )(x, *args)


def _route_body(x, g_ref, wr_ref, br_ref, tri_ref, tbl_ref, h3_ref, wts_ref, meta_ref, cnt_ref, cnt_scr):
    t, d = x.shape

    @pl.when(pl.program_id(0) == 0)
    def _():
        cnt_scr[...] = jnp.zeros(cnt_scr.shape, F32)

    h = _rms(x, g_ref[...])
    lg = jnp.dot(h.astype(BF16), wr_ref[...], preferred_element_type=F32) + br_ref[...]
    lane = lax.broadcasted_iota(I32, (t, LANES), 1)
    ninf = -jnp.inf

    gmask = lane < N_GROUPS
    gl = jnp.where(gmask, lg, ninf)
    gmax = jnp.max(gl, axis=1, keepdims=True)
    gidx = jnp.min(jnp.where(gl == gmax, lane, LANES), axis=1, keepdims=True)
    gsum = jnp.sum(jnp.where(gmask, jnp.exp(lg - gmax), 0.0), axis=1, keepdims=True)
    gw = 1.0 / gsum

    lo = N_GROUPS + gidx * N_EXP
    el = jnp.where(lane >= lo, jnp.where(lane < lo + N_EXP, lg, ninf), ninf)
    m1 = jnp.max(el, axis=1, keepdims=True)
    i1 = jnp.min(jnp.where(el == m1, lane, LANES), axis=1, keepdims=True)
    el2 = jnp.where(lane == i1, ninf, el)
    m2 = jnp.max(el2, axis=1, keepdims=True)
    i2 = jnp.min(jnp.where(el2 == m2, lane, LANES), axis=1, keepdims=True)
    dlt = jnp.exp(m2 - m1)
    p1 = 1.0 / (1.0 + dlt)
    w1 = p1 * gw
    w2 = dlt * p1 * gw

    q = (i1 - lo) * N_EXP + (i2 - lo)
    qoh = lane == q
    cl = jnp.sum(jnp.where(qoh, tbl_ref[0:1, :], 0.0), axis=1, keepdims=True)
    fa = jnp.sum(jnp.where(qoh, tbl_ref[1:2, :], 0.0), axis=1, keepdims=True)
    cls = gidx * N_PAIRS + cl.astype(I32)
    first_a = fa > 0.5
    wa = jnp.where(first_a, w1, w2)
    wb = jnp.where(first_a, w2, w1)

    oh = lane == cls
    cum = jnp.dot(tri_ref[...], jnp.where(oh, 1.0, 0.0).astype(BF16), preferred_element_type=F32)
    cnt = cnt_scr[...]
    rank = jnp.sum(jnp.where(oh, cum + cnt, 0.0), axis=1, keepdims=True) - 1.0
    cnt = cnt + cum[t - 1:t, :]
    cnt_scr[...] = cnt
    cnt_ref[...] = cnt

    for s in range(d // LANES):
        h3_ref[pl.ds(s, t, stride=SUBLANES), :] = h[:, s * LANES:(s + 1) * LANES]
    lane8 = lax.broadcasted_iota(I32, (t, 8), 1)
    wts_ref[...] = jnp.where(lane8 == 0, wa, jnp.where(lane8 == 1, wb, 0.0))
    meta_ref[...] = jnp.where(lane8 == 0, cls, jnp.where(lane8 == 1, rank.astype(I32), 0))


def _route0_kernel(x_ref, g_ref, wr_ref, br_ref, tri_ref, tbl_ref, h3_ref, wts_ref, meta_ref, cnt_ref, cnt_scr):
    _route_body(x_ref[...], g_ref, wr_ref, br_ref, tri_ref, tbl_ref, h3_ref, wts_ref, meta_ref, cnt_ref, cnt_scr)


def _route1_kernel(x_ref, o_ref, wo_ref, g_ref, wr_ref, br_ref, tri_ref, tbl_ref,
                   xo_ref, h3_ref, wts_ref, meta_ref, cnt_ref, cnt_scr):
    x = x_ref[...] + jnp.dot(o_ref[...], wo_ref[...], preferred_element_type=F32)
    xo_ref[...] = x
    _route_body(x, g_ref, wr_ref, br_ref, tri_ref, tbl_ref, h3_ref, wts_ref, meta_ref, cnt_ref, cnt_scr)


def _route(x, g, w_rg, b_rg, w_re, b_re, attn_o=None, w_o=None):
    n, d = x.shape
    assert d == SUBLANES * LANES, "one (8, 128) tile per token row"
    t = min(TILE_ROUTE, n)
    wr = jnp.zeros((d, LANES), F32)
    wr = wr.at[:, :N_GROUPS].set(w_rg)
    wr = wr.at[:, N_GROUPS:N_GROUPS + N_GROUPS * N_EXP].set(jnp.transpose(w_re, (1, 0, 2)).reshape(d, -1))
    br = jnp.zeros((1, LANES), F32)
    br = br.at[0, :N_GROUPS].set(b_rg)
    br = br.at[0, N_GROUPS:N_GROUPS + N_GROUPS * N_EXP].set(b_re.reshape(-1))
    tri = jnp.asarray(np.tril(np.ones((t, t), np.float32)), BF16)
    tbl = jnp.asarray(_PAIR_TBL)
    full = lambda a: pl.BlockSpec(a.shape, lambda i: (0,) * a.ndim)
    rows = lambda w: pl.BlockSpec((t, w), lambda i: (i, 0))
    common = (g.reshape(1, -1), wr.astype(BF16), br, tri, tbl)
    out_shape = [jax.ShapeDtypeStruct((n * SUBLANES, LANES), F32), jax.ShapeDtypeStruct((n, 8), F32),
                 jax.ShapeDtypeStruct((n, 8), I32), jax.ShapeDtypeStruct((1, LANES), F32)]
    out_specs = [pl.BlockSpec((t * SUBLANES, LANES), lambda i: (i, 0)), rows(8), rows(8),
                 pl.BlockSpec((1, LANES), lambda i: (0, 0))]
    if attn_o is None:
        h3, wts, meta, cnt = pl.pallas_call(
            _route0_kernel, out_shape=out_shape, grid=(n // t,),
            in_specs=[rows(d)] + [full(a) for a in common], out_specs=out_specs,
            scratch_shapes=[pltpu.VMEM((1, LANES), F32)],
            compiler_params=_params("arbitrary"), name="moe_route",
        )(x, *common)
        return x, h3, wts, meta, cnt
    wo = w_o.astype(BF16)
    xo, h3, wts, meta, cnt = pl.pallas_call(
        _route1_kernel, out_shape=[jax.ShapeDtypeStruct((n, d), F32)] + out_shape, grid=(n // t,),
        in_specs=[rows(d), rows(attn_o.shape[1]), full(wo)] + [full(a) for a in common],
        out_specs=[rows(d)] + out_specs,
        scratch_shapes=[pltpu.VMEM((1, LANES), F32)],
        compiler_params=_params("arbitrary"), name="attn_out_moe_route",
    )(x, attn_o, wo, *common)
    return xo, h3, wts, meta, cnt


def _plan(meta, cnt, n):
    tm = TILE_EXP
    nt_max = n // tm + N_CLASSES
    cnt = cnt[0, :N_CLASSES].astype(I32)
    nt = (cnt + tm - 1) // tm
    tend = jnp.cumsum(nt)
    offs = (tend - nt) * tm
    total = tend[-1]
    classes = jnp.arange(N_CLASSES, dtype=I32)
    dest = jnp.sum(jnp.where(meta[:, 0:1] == classes[None, :], offs[None, :], 0), axis=1) + meta[:, 1]
    tile = jnp.arange(nt_max, dtype=I32)
    rowblk = jnp.minimum(tile, total - 1)
    tcls = jnp.sum((tend[None, :] <= rowblk[:, None]).astype(I32), axis=1)
    active = (tile < total).astype(I32)
    exp_a = jnp.take(jnp.asarray(_CLASS_EXP_A), tcls)
    exp_b = jnp.take(jnp.asarray(_CLASS_EXP_B), tcls)
    pad = jnp.zeros((LANES,), I32)
    padstart = pad.at[:N_CLASSES].set(offs + cnt)
    padstart = padstart.at[N_CLASSES].set(total * 2)
    padlen = pad.at[:N_CLASSES].set(nt * tm - cnt)
    return dest, rowblk, active, exp_a, exp_b, padstart, padlen, nt_max


_PAD_BITS = tuple(1 << b for b in reversed(range(int(math.log2(TILE_EXP)))))
DMA_UNROLL = 8


def _dispatch_kernel(dest_ref, padstart_ref, padlen_ref, h3_ref, hs_ref, zbuf, sem):
    rows = h3_ref.shape[0]
    t = rows // SUBLANES
    i = pl.program_id(0)

    def issue(g, carry):
        for u in range(DMA_UNROLL):
            r = g * DMA_UNROLL + u
            d = dest_ref[i * t + r]
            pltpu.make_async_copy(h3_ref.at[pl.ds(pl.multiple_of(r * SUBLANES, SUBLANES), SUBLANES)],
                                  hs_ref.at[pl.ds(pl.multiple_of(d * SUBLANES, SUBLANES), SUBLANES)],
                                  sem.at[0]).start(priority=u % 2)
        return carry

    lax.fori_loop(0, t // DMA_UNROLL, issue, 0)
    pltpu.make_async_copy(h3_ref, hs_ref.at[pl.ds(0, rows)], sem.at[0]).wait()

    @pl.when(i == pl.num_programs(0) - 1)
    def _():
        zbuf[...] = jnp.zeros(zbuf.shape, F32)

        def pad_copy(start, size):
            return pltpu.make_async_copy(zbuf.at[pl.ds(0, size * SUBLANES)],
                                         hs_ref.at[pl.ds(pl.multiple_of(start * SUBLANES, SUBLANES), size * SUBLANES)],
                                         sem.at[1])

        def pieces(c, fn):
            start = padstart_ref[c]
            left = padlen_ref[c]
            for bit in _PAD_BITS:
                has = (left & bit) != 0

                @pl.when(has)
                def _(start=start, bit=bit):
                    fn(pad_copy(start, bit))

                start = start + jnp.where(has, bit, 0)

        def issue_pad(c, carry):
            pieces(c, lambda cp: cp.start())
            return carry

        def drain_pad(c, carry):
            pieces(c, lambda cp: cp.wait())
            return carry

        lax.fori_loop(0, N_CLASSES, issue_pad, 0)
        lax.fori_loop(0, N_CLASSES, drain_pad, 0)

        half = zbuf.shape[0] // SUBLANES
        first_unused = padstart_ref[N_CLASSES]
        n_halves = hs_ref.shape[0] // zbuf.shape[0]

        def issue_tail(j, carry):
            pad_copy(j * half, half).start()
            return carry

        def drain_tail(j, carry):
            pad_copy(j * half, half).wait()
            return carry

        lax.fori_loop(first_unused, n_halves, issue_tail, 0)
        lax.fori_loop(first_unused, n_halves, drain_tail, 0)


def _dispatch(h3, dest, padstart, padlen, nt_max):
    n = h3.shape[0] // SUBLANES
    t = min(TILE_DISP, n)
    return pl.pallas_call(
        _dispatch_kernel,
        out_shape=jax.ShapeDtypeStruct((nt_max * TILE_EXP * SUBLANES, LANES), F32),
        grid_spec=pltpu.PrefetchScalarGridSpec(
            num_scalar_prefetch=3, grid=(n // t,),
            in_specs=[pl.BlockSpec((t * SUBLANES, LANES), lambda i, *_: (i, 0))],
            out_specs=pl.BlockSpec(memory_space=pl.ANY),
            scratch_shapes=[pltpu.VMEM((TILE_EXP // 2 * SUBLANES, LANES), F32), pltpu.SemaphoreType.DMA((2,))]),
        compiler_params=_params("arbitrary"), name="moe_dispatch",
    )(dest, padstart, padlen, h3)


def _expert_kernel(rowblk_ref, active_ref, ea_ref, eb_ref, hs_ref,
                   wga_ref, wua_ref, wda_ref, wgb_ref, wub_ref, wdb_ref, ys_ref, rec_scr):
    tm = hs_ref.shape[0] // SUBLANES
    i = pl.program_id(0)

    @pl.when(active_ref[i] == 1)
    def _():
        x = jnp.concatenate([hs_ref[pl.ds(s, tm, stride=SUBLANES), :] for s in range(SUBLANES)], axis=1).astype(BF16)
        for slot, (wg, wu, wd) in enumerate(((wga_ref, wua_ref, wda_ref), (wgb_ref, wub_ref, wdb_ref))):
            hg = jnp.dot(x, wg[0], preferred_element_type=F32)
            ug = jnp.dot(x, wu[0], preferred_element_type=F32)
            act = hg * jax.nn.sigmoid(hg) * ug
            y = jnp.dot(act.astype(BF16), wd[0], preferred_element_type=F32)
            for s in range(SUBLANES):
                rec_scr[slot, pl.ds(s, tm, stride=SUBLANES), :] = y[:, s * LANES:(s + 1) * LANES]
        rec = jnp.stack([rec_scr[0].reshape(tm, SUBLANES, LANES), rec_scr[1].reshape(tm, SUBLANES, LANES)], axis=1)
        ys_ref[...] = rec.reshape(tm * 2 * SUBLANES, LANES).astype(ys_ref.dtype)

    @pl.when(active_ref[i] == 0)
    def _():
        ys_ref[...] = jnp.zeros(ys_ref.shape, ys_ref.dtype)


def _experts(hs, rowblk, active, exp_a, exp_b, w_gate, w_up, w_down):
    tm = TILE_EXP
    rows = hs.shape[0] // SUBLANES
    d = SUBLANES * LANES
    f = w_gate.shape[-1]
    wg = w_gate.reshape(-1, d, f).astype(BF16)
    wu = w_up.reshape(-1, d, f).astype(BF16)
    wd = w_down.reshape(-1, f, d).astype(BF16)
    in_a = lambda shp: pl.BlockSpec((1,) + shp, lambda i, rb, ac, ea, eb: (ea[i], 0, 0))
    in_b = lambda shp: pl.BlockSpec((1,) + shp, lambda i, rb, ac, ea, eb: (eb[i], 0, 0))
    return pl.pallas_call(
        _expert_kernel,
        out_shape=jax.ShapeDtypeStruct((rows * 2 * SUBLANES, LANES), BF16),
        grid_spec=pltpu.PrefetchScalarGridSpec(
            num_scalar_prefetch=4, grid=(rows // tm,),
            in_specs=[pl.BlockSpec((tm * SUBLANES, LANES), lambda i, rb, ac, ea, eb: (rb[i], 0)),
                      in_a((d, f)), in_a((d, f)), in_a((f, d)), in_b((d, f)), in_b((d, f)), in_b((f, d))],
            out_specs=pl.BlockSpec((tm * 2 * SUBLANES, LANES), lambda i, *_: (i, 0)),
            scratch_shapes=[pltpu.VMEM((2, tm * SUBLANES, LANES), F32)]),
        compiler_params=_params("arbitrary"), name="moe_experts",
    )(rowblk, active, exp_a, exp_b, hs, wg, wu, wd, wg, wu, wd)


def _ple_kernel(dest_ref, x_ref, wts_ref, p_ref, g_ref, wg_ref, wp_ref, fg_ref, ys_ref, o_ref, ybuf0, ybuf1, rec_scr,
                sem, *, final):
    t = x_ref.shape[0]
    rec = 2 * SUBLANES
    i = pl.program_id(0)
    last = pl.num_programs(0) - 1

    def record_copy(d, row, buf, s):
        return pltpu.make_async_copy(ys_ref.at[pl.ds(pl.multiple_of(d * rec, rec), rec)],
                                     buf.at[pl.ds(row, rec)], sem.at[s])

    def tile_wait(buf, s):
        pltpu.make_async_copy(ys_ref.at[pl.ds(0, t * rec)], buf, sem.at[s]).wait()

    @pl.when(i == 0)
    def _():
        def issue(r, carry):
            record_copy(dest_ref[r], pl.multiple_of(r * rec, rec), ybuf0, 0).start()
            return carry
        lax.fori_loop(0, t, issue, 0, unroll=DMA_UNROLL)

    def step(cur, cs, nxt, ns):
        tile_wait(cur, cs)
        base = jnp.minimum(i + 1, last) * t
        for r in range(t):
            record_copy(dest_ref[base + r], r * rec, nxt, ns).start()

        recs = cur[...].astype(F32).reshape(t, 2, SUBLANES, LANES)
        wts = wts_ref[...]
        x = x_ref[...]
        for e in range(2):
            rec_scr[e] = recs[:, e].reshape(t * SUBLANES, LANES)
            ye = jnp.concatenate([rec_scr[e, pl.ds(s, t, stride=SUBLANES), :] for s in range(SUBLANES)], axis=1)
            x = x + wts[:, e:e + 1] * ye
        gate = jax.nn.sigmoid(jnp.dot(_rms(x, g_ref[...]).astype(BF16), wg_ref[...], preferred_element_type=F32))
        x = x + gate * jnp.dot(p_ref[0].astype(BF16), wp_ref[...], preferred_element_type=F32)
        if final:
            x = _rms(x, fg_ref[...])
        o_ref[...] = x

        @pl.when(i == last)
        def _():
            tile_wait(nxt, ns)

    @pl.when(i % 2 == 0)
    def _():
        step(ybuf0, 0, ybuf1, 1)

    @pl.when(i % 2 == 1)
    def _():
        step(ybuf1, 1, ybuf0, 0)


def _combine_ple(x, ys, dest, wts, p, layer, g, w_gate, w_proj, final_g, final):
    n, d = x.shape
    t = min(TILE_PLE, n)
    rec = 2 * SUBLANES
    full = lambda a: pl.BlockSpec(a.shape, lambda i, *_: (0,) * a.ndim)
    rows = lambda w: pl.BlockSpec((t, w), lambda i, *_: (i, 0))
    args = (g.reshape(1, -1), w_gate.astype(BF16), w_proj.astype(BF16), final_g.reshape(1, -1))
    return pl.pallas_call(
        functools.partial(_ple_kernel, final=final),
        out_shape=jax.ShapeDtypeStruct((n, d), F32),
        grid_spec=pltpu.PrefetchScalarGridSpec(
            num_scalar_prefetch=1, grid=(n // t,),
            in_specs=[rows(d), rows(wts.shape[1]), pl.BlockSpec((1, t, p.shape[2]), lambda i, *_: (layer, i, 0))]
                     + [full(a) for a in args] + [pl.BlockSpec(memory_space=pl.ANY)],
            out_specs=rows(d),
            scratch_shapes=[pltpu.VMEM((t * rec, LANES), ys.dtype), pltpu.VMEM((t * rec, LANES), ys.dtype),
                            pltpu.VMEM((2, t * SUBLANES, LANES), F32), pltpu.SemaphoreType.DMA((2,))]),
        compiler_params=_params("arbitrary"), name="moe_combine_ple",
    )(dest, x, wts, p, *args, ys)


def _moe_ple(x, p, layer, g_ffn, w_rg, b_rg, w_re, b_re, w_gate, w_up, w_down, ple_g, ple_wg, ple_wp, final_g, final,
             attn_o=None, w_o=None):
    n = x.shape[0]
    x, h3, wts, meta, cnt = _route(x, g_ffn, w_rg, b_rg, w_re, b_re, attn_o, w_o)
    dest, rowblk, active, exp_a, exp_b, padstart, padlen, nt_max = _plan(meta, cnt, n)
    hs = _dispatch(h3, dest, padstart, padlen, nt_max)
    ys = _experts(hs, rowblk, active, exp_a, exp_b, w_gate, w_up, w_down)
    return _combine_ple(x, ys, dest, wts, p, layer, ple_g, ple_wg, ple_wp, final_g, final)


def _qkv_kernel(x_ref, pos_ref, g_ref, wq_ref, wk_ref, wv_ref, invf_ref, q_ref, k_ref, vt_ref, *, qscale):
    t = x_ref.shape[1]
    h = _rms(x_ref[0], g_ref[...]).astype(BF16)
    lane = lax.broadcasted_iota(I32, (t, LANES), 1) % HEAD_DIM
    ang = pos_ref[0].astype(F32) * invf_ref[...]
    cos = jnp.cos(ang)
    sin = jnp.sin(ang)
    half = ROPE_DIM // 2
    cmul = jnp.where(lane < ROPE_DIM, cos, 1.0)
    s_up = jnp.where(lane < half, -sin, 0.0)
    s_dn = jnp.where(lane >= half, jnp.where(lane < ROPE_DIM, sin, 0.0), 0.0)

    def rope(w_ref, o_ref, scale):
        y = jnp.dot(h, w_ref[...], preferred_element_type=F32)
        for c0 in range(0, y.shape[1], LANES):
            yb = y[:, c0:c0 + LANES]
            rot = yb * cmul + pltpu.roll(yb, LANES - half, 1) * s_up + pltpu.roll(yb, half, 1) * s_dn
            o_ref[0, :, c0:c0 + LANES] = (rot * scale).astype(o_ref.dtype)

    rope(wq_ref, q_ref, qscale)
    rope(wk_ref, k_ref, 1.0)
    v = jnp.dot(h, wv_ref[...], preferred_element_type=F32)
    vt_ref[0] = v.T.astype(vt_ref.dtype)


def _qkv(x, positions, g, w_qkv):
    b, s, d = x.shape
    t = min(TILE_QKV, s)
    nq = (w_qkv.shape[1] - d) // 2
    w = w_qkv.astype(BF16)
    wq, wk, wv = w[:, :nq], w[:, nq:2 * nq], w[:, 2 * nq:]
    inv_freq = ROPE_THETA ** (-jnp.arange(0, ROPE_DIM, 2, dtype=F32) / ROPE_DIM)
    lane = np.arange(LANES) % HEAD_DIM
    invf = jnp.where(lane < ROPE_DIM, jnp.take(inv_freq, lane % (ROPE_DIM // 2)), 0.0).reshape(1, LANES)
    qscale = HEAD_DIM ** -0.5 * math.log2(math.e)
    full = lambda a: pl.BlockSpec(a.shape, lambda i, j: (0,) * a.ndim)
    args = (g.reshape(1, -1), wq, wk, wv, invf)
    return pl.pallas_call(
        functools.partial(_qkv_kernel, qscale=qscale),
        out_shape=[jax.ShapeDtypeStruct((b, s, nq), BF16), jax.ShapeDtypeStruct((b, s, nq), BF16),
                   jax.ShapeDtypeStruct((b, d, s), BF16)],
        grid=(b, s // t),
        in_specs=[pl.BlockSpec((1, t, d), lambda i, j: (i, j, 0)), pl.BlockSpec((1, t, 1), lambda i, j: (i, j, 0))]
                 + [full(a) for a in args],
        out_specs=[pl.BlockSpec((1, t, nq), lambda i, j: (i, j, 0)), pl.BlockSpec((1, t, nq), lambda i, j: (i, j, 0)),
                   pl.BlockSpec((1, d, t), lambda i, j: (i, 0, j))],
        compiler_params=_params("arbitrary", "arbitrary"), name="qkv_rope",
    )(x, positions.reshape(b, s, 1), *args)


def _attn_kernel(q_ref, k_ref, vt_ref, lam_ref, sg_ref, o_ref, m_scr, l_scr, acc_scr, st_a, st_b, *, lam_init):
    tq = q_ref.shape[1]
    tk = tq
    qi = pl.program_id(2)
    q = q_ref[0]
    qs = [q[:, c * HEAD_DIM:(c + 1) * HEAD_DIM] for c in range(2)]
    m_scr[...] = jnp.full(m_scr.shape, -1e30, F32)
    l_scr[...] = jnp.zeros(l_scr.shape, F32)
    acc_scr[...] = jnp.zeros(acc_scr.shape, F32)

    def scores(buf, j):
        kb = k_ref[0, pl.ds(pl.multiple_of(j * tk, tk), tk), :]
        for c in range(2):
            buf[c] = lax.dot_general(kb[:, c * HEAD_DIM:(c + 1) * HEAD_DIM], qs[c], (((1,), (1,)), ((), ())),
                                     preferred_element_type=F32)

    def accumulate(buf, j, masked):
        parts = ((0, tk // 2, 0), (tk // 2, tk // 2, tq // 2)) if masked else ((0, tk, 0),)
        for k_lo, k_len, q_lo in parts:
            vtb = vt_ref[0, :, pl.ds(pl.multiple_of(j * tk + k_lo, k_len), k_len)]
            if masked:
                keep = (lax.broadcasted_iota(I32, (k_len, tq - q_lo), 1) + q_lo
                        >= lax.broadcasted_iota(I32, (k_len, tq - q_lo), 0) + k_lo)
            for c in range(2):
                st = buf[c, k_lo:k_lo + k_len, q_lo:]
                if masked:
                    st = jnp.where(keep, st, -1e30)
                m_old = m_scr[c:c + 1, q_lo:]
                m_new = jnp.maximum(m_old, jnp.max(st, axis=0, keepdims=True))
                alpha = jnp.exp2(m_old - m_new)
                p = jnp.exp2(st - m_new)
                l_scr[c:c + 1, q_lo:] = alpha * l_scr[c:c + 1, q_lo:] + jnp.sum(p, axis=0, keepdims=True)
                m_scr[c:c + 1, q_lo:] = m_new
                acc_scr[c, :, q_lo:] = (alpha * acc_scr[c, :, q_lo:]
                                        + jnp.dot(vtb, p.astype(BF16), preferred_element_type=F32))

    scores(st_a, 0)

    def pair(jp, carry):
        j = 2 * jp
        scores(st_b, j + 1)
        accumulate(st_a, j, False)
        scores(st_a, j + 2)
        accumulate(st_b, j + 1, False)
        return carry

    lax.fori_loop(0, qi // 2, pair, 0)

    @pl.when(qi % 2 == 1)
    def _():
        scores(st_b, qi)
        accumulate(st_a, qi - 1, False)
        accumulate(st_b, qi, True)

    @pl.when(qi % 2 == 0)
    def _():
        accumulate(st_a, qi, True)

    lp = lam_ref[...]
    lam = (jnp.exp(jnp.sum(lp[0:1] * lp[1:2], axis=1, keepdims=True))
           - jnp.exp(jnp.sum(lp[2:3] * lp[3:4], axis=1, keepdims=True)) + lam_init)
    ot = acc_scr[0] / l_scr[0:1, :] - lam * (acc_scr[1] / l_scr[1:2, :])
    ot = ot * lax.rsqrt(jnp.mean(ot * ot, axis=0, keepdims=True) + EPS) * sg_ref[...] * (1.0 - lam_init)
    o_ref[0] = ot.T.astype(o_ref.dtype)


def _attention(q, k, vt, lam_params, subln_g, lam_init):
    b, s, nq = q.shape
    heads = nq // (2 * HEAD_DIM)
    tq = min(TILE_Q, s)
    return pl.pallas_call(
        functools.partial(_attn_kernel, lam_init=lam_init),
        out_shape=jax.ShapeDtypeStruct((b, s, heads * V_DIM), BF16),
        grid=(b, heads, s // tq),
        in_specs=[pl.BlockSpec((1, tq, 2 * HEAD_DIM), lambda i, h, j: (i, j, h)),
                  pl.BlockSpec((1, s, 2 * HEAD_DIM), lambda i, h, j: (i, 0, h)),
                  pl.BlockSpec((1, V_DIM, s), lambda i, h, j: (i, h, 0)),
                  pl.BlockSpec(lam_params.shape, lambda i, h, j: (0, 0)),
                  pl.BlockSpec((V_DIM, 1), lambda i, h, j: (0, 0))],
        out_specs=pl.BlockSpec((1, tq, V_DIM), lambda i, h, j: (i, j, h)),
        scratch_shapes=[pltpu.VMEM((2, tq), F32), pltpu.VMEM((2, tq), F32), pltpu.VMEM((2, V_DIM, tq), F32),
                        pltpu.VMEM((2, tq, tq), F32), pltpu.VMEM((2, tq, tq), F32)],
        compiler_params=_params("arbitrary", "arbitrary", "arbitrary"), name="diff_attention",
    )(q, k, vt, lam_params, subln_g.reshape(V_DIM, 1))


def kernel(x, p, positions, norm_mix, norm_ffn, conv_w_pw1, conv_b_pw1, conv_w_dw, conv_b_dw, conv_ln_g, conv_ln_b,
           conv_w_pw2, conv_b_pw2, da_w_qkv, da_lambda, da_subln, da_w_o, moe_w_rg, moe_b_rg, moe_w_re, moe_b_re,
           moe_w_gate, moe_w_up, moe_w_down, ple_norm, ple_w_gate, ple_w_proj, final_norm):
    b, s, d = x.shape
    n = b * s
    depth = norm_mix.shape[0]
    assert depth == 2, "layer 0 is the conv mixer, layer 1 differential attention"
    pf = p.reshape(depth, n, -1)

    def moe(i, xin, final, attn_o=None, w_o=None):
        return _moe_ple(xin, pf, i, norm_ffn[i], moe_w_rg[i], moe_b_rg[i], moe_w_re[i], moe_b_re[i],
                        moe_w_gate[i], moe_w_up[i], moe_w_down[i], ple_norm[i], ple_w_gate[i], ple_w_proj[i],
                        final_norm, final, attn_o, w_o)

    x = _conv_mixer(x, norm_mix[0], conv_w_pw1[0], conv_b_pw1[0], conv_w_dw[0], conv_b_dw[0],
                    conv_ln_g[0], conv_ln_b[0], conv_w_pw2[0], conv_b_pw2[0])
    x = moe(0, x.reshape(n, d), False)

    lam_init = 0.8 - 0.6 * math.exp(-0.3 * 1)
    q, k, vt = _qkv(x.reshape(b, s, d), positions, norm_mix[1], da_w_qkv[0])
    o = _attention(q, k, vt, da_lambda[0], da_subln[0], lam_init)
    x = moe(1, x, True, o.reshape(n, -1), da_w_o[0])
    return x.reshape(b, s, d)
```

```python
import functools
import math

import numpy as np
import jax
import jax.numpy as jnp
from jax import lax
from jax.experimental import pallas as pl
from jax.experimental.pallas import tpu as pltpu

F32 = jnp.float32
BF16 = jnp.bfloat16
I32 = jnp.int32

EPS = 1e-6
CONV_WIDTH = 31
CONV_HALO = 32
HEAD_DIM = 64
V_DIM = 128
ROPE_DIM = 16
ROPE_THETA = 500000.0
N_GROUPS = 4
N_EXP = 8
N_PAIRS = N_EXP * (N_EXP - 1) // 2
N_CLASSES = N_GROUPS * N_PAIRS
LANES = 128
SUBLANES = 8
VMEM_LIMIT = 56 * 1024 * 1024

TILE_CONV = 512
TILE_ROUTE = 512
TILE_DISP = 2048
TILE_EXP = 256
TILE_PLE = 512
TILE_QKV = 512
TILE_Q = 512
TILE_K = 256
CONV_GROUP = 8
CONV_GAP = 4


def _pair_tables():
    remaining = list(range(N_EXP))
    order = []
    center, center_slot = 0, 0
    while len(remaining) > 1:
        others = [v for v in remaining if v != center]
        for v in others:
            order.append((center, v) if center_slot == 0 else (v, center))
        remaining.remove(center)
        center, center_slot = others[-1], 1 - center_slot
    assert len(order) == N_PAIRS and len({frozenset(p) for p in order}) == N_PAIRS
    cls = np.zeros((N_EXP * N_EXP,), np.float32)
    first_is_a = np.zeros((N_EXP * N_EXP,), np.float32)
    for idx, (a, b) in enumerate(order):
        cls[a * N_EXP + b] = idx
        cls[b * N_EXP + a] = idx
        first_is_a[a * N_EXP + b] = 1.0
    tbl = np.zeros((8, LANES), np.float32)
    tbl[0, : N_EXP * N_EXP] = cls
    tbl[1, : N_EXP * N_EXP] = first_is_a
    exp_a = np.array([g * N_EXP + a for g in range(N_GROUPS) for (a, _) in order], np.int32)
    exp_b = np.array([g * N_EXP + b for g in range(N_GROUPS) for (_, b) in order], np.int32)
    return tbl, exp_a, exp_b


_PAIR_TBL, _CLASS_EXP_A, _CLASS_EXP_B = _pair_tables()


def _rms(x, g):
    return x * lax.rsqrt(jnp.mean(x * x, axis=-1, keepdims=True) + EPS) * g


def _params(*sem):
    return pltpu.CompilerParams(dimension_semantics=sem, vmem_limit_bytes=VMEM_LIMIT)


def _conv_kernel(x_ref, g_ref, w1_ref, b1_ref, w3_ref, bdw_ref, lng_ref, lnb_ref, w2_ref, b2_ref,
                 o_ref, vbuf, cbuf):
    nslab, t, _ = cbuf.shape
    c = nslab * LANES
    seg = t // SUBLANES
    pitch = CONV_HALO + seg + CONV_GAP
    x = x_ref[0]
    h = _rms(x, g_ref[...])
    u = jnp.dot(h.astype(BF16), w1_ref[...], preferred_element_type=F32) + b1_ref[...]
    v = u[:, :c] * jax.nn.sigmoid(u[:, c:])

    last = (SUBLANES - 1) * pitch + seg
    @pl.when(pl.program_id(1) == 0)
    def _():
        vbuf[:, 0:CONV_HALO, :] = jnp.zeros((nslab, CONV_HALO, LANES), F32)

    @pl.when(pl.program_id(1) > 0)
    def _():
        vbuf[:, 0:CONV_HALO, :] = vbuf[:, last:last + CONV_HALO, :]

    for s in range(nslab):
        vs = v[:, s * LANES:(s + 1) * LANES]
        for j in range(SUBLANES):
            if j > 0:
                vbuf[s, j * pitch:j * pitch + CONV_HALO, :] = vs[j * seg - CONV_HALO:j * seg]
            vbuf[s, j * pitch + CONV_HALO:j * pitch + CONV_HALO + seg, :] = vs[j * seg:(j + 1) * seg]

    first = CONV_HALO - (CONV_WIDTH - 1)
    for s in range(nslab):
        wv = [jnp.broadcast_to(w3_ref[s, k:k + 1, :], (SUBLANES, LANES)) for k in range(CONV_WIDTH)]
        bias = jnp.broadcast_to(bdw_ref[:, s * LANES:(s + 1) * LANES], (SUBLANES, LANES))

        def step(g, carry, s=s, wv=wv, bias=bias):
            i0 = g * CONV_GROUP
            accs = [[bias, None] for _ in range(CONV_GROUP)]
            for off in range(CONV_GROUP + CONV_WIDTH - 1):
                val = vbuf[s, pl.ds(i0 + first + off, SUBLANES, stride=pitch), :]
                for u in range(CONV_GROUP):
                    k = off - u
                    if 0 <= k < CONV_WIDTH:
                        prod = wv[k] * val
                        accs[u][k % 2] = prod if accs[u][k % 2] is None else accs[u][k % 2] + prod
            for u in range(CONV_GROUP):
                cbuf[s, pl.ds(pl.multiple_of((i0 + u) * SUBLANES, SUBLANES), SUBLANES), :] = accs[u][0] + accs[u][1]
            return carry

        lax.fori_loop(0, seg // CONV_GROUP, step, 0)

    y = jnp.concatenate(
        [jnp.concatenate([cbuf[s, pl.ds(j, seg, stride=SUBLANES), :] for j in range(SUBLANES)], axis=0)
         for s in range(nslab)], axis=1)
    mu = jnp.mean(y, axis=-1, keepdims=True)
    d = y - mu
    var = jnp.mean(d * d, axis=-1, keepdims=True)
    y = d * lax.rsqrt(var + EPS) * lng_ref[...] + lnb_ref[...]
    y = y * jax.nn.sigmoid(y)
    o_ref[0] = x + jnp.dot(y.astype(BF16), w2_ref[...], preferred_element_type=F32) + b2_ref[...]


def _conv_mixer(x, g, w1, b1, wdw, bdw, lng, lnb, w2, b2):
    b, s, d = x.shape
    c = w2.shape[0]
    t = min(TILE_CONV, s)
    seg = t // SUBLANES
    assert seg >= CONV_HALO and seg % SUBLANES == 0 and c % LANES == 0
    nslab = c // LANES
    pitch = CONV_HALO + seg + CONV_GAP
    row = lambda a: a.reshape(1, -1)
    full = lambda a: pl.BlockSpec(a.shape, lambda i, j: (0,) * a.ndim)
    w3 = jnp.pad(wdw, ((0, CONV_HALO - CONV_WIDTH), (0, 0))).reshape(CONV_HALO, nslab, LANES).transpose(1, 0, 2)
    args = (row(g), w1.astype(BF16), row(b1), w3, row(bdw), row(lng), row(lnb), w2.astype(BF16), row(b2))
    return pl.pallas_call(
        _conv_kernel,
        out_shape=jax.ShapeDtypeStruct(x.shape, F32),
        grid=(b, s // t),
        in_specs=[pl.BlockSpec((1, t, d), lambda i, j: (i, j, 0))] + [full(a) for a in args],
        out_specs=pl.BlockSpec((1, t, d), lambda i, j: (i, j, 0)),
        scratch_shapes=[pltpu.VMEM((nslab, SUBLANES * pitch, LANES), F32), pltpu.VMEM((nslab, t, LANES), F32)],
        compiler_params=_params("arbitrary", "arbitrary"),
        name="conv_mixer",
    )(x, *args)


def _route_body(x, g_ref, wr_ref, br_ref, tri_ref, tbl_ref, h3_ref, wts_ref, meta_ref, cnt_ref, cnt_scr):
    t, d = x.shape

    @pl.when(pl.program_id(0) == 0)
    def _():
        cnt_scr[...] = jnp.zeros(cnt_scr.shape, F32)

    h = _rms(x, g_ref[...])
    lg = jnp.dot(h.astype(BF16), wr_ref[...], preferred_element_type=F32) + br_ref[...]
    lane = lax.broadcasted_iota(I32, (t, LANES), 1)
    ninf = -jnp.inf

    gmask = lane < N_GROUPS
    gl = jnp.where(gmask, lg, ninf)
    gmax = jnp.max(gl, axis=1, keepdims=True)
    gidx = jnp.min(jnp.where(gl == gmax, lane, LANES), axis=1, keepdims=True)
    gsum = jnp.sum(jnp.where(gmask, jnp.exp(lg - gmax), 0.0), axis=1, keepdims=True)
    gw = 1.0 / gsum

    lo = N_GROUPS + gidx * N_EXP
    el = jnp.where(lane >= lo, jnp.where(lane < lo + N_EXP, lg, ninf), ninf)
    m1 = jnp.max(el, axis=1, keepdims=True)
    i1 = jnp.min(jnp.where(el == m1, lane, LANES), axis=1, keepdims=True)
    el2 = jnp.where(lane == i1, ninf, el)
    m2 = jnp.max(el2, axis=1, keepdims=True)
    i2 = jnp.min(jnp.where(el2 == m2, lane, LANES), axis=1, keepdims=True)
    dlt = jnp.exp(m2 - m1)
    p1 = 1.0 / (1.0 + dlt)
    w1 = p1 * gw
    w2 = dlt * p1 * gw

    q = (i1 - lo) * N_EXP + (i2 - lo)
    qoh = lane == q
    cl = jnp.sum(jnp.where(qoh, tbl_ref[0:1, :], 0.0), axis=1, keepdims=True)
    fa = jnp.sum(jnp.where(qoh, tbl_ref[1:2, :], 0.0), axis=1, keepdims=True)
    cls = gidx * N_PAIRS + cl.astype(I32)
    first_a = fa > 0.5
    wa = jnp.where(first_a, w1, w2)
    wb = jnp.where(first_a, w2, w1)

    oh = lane == cls
    cum = jnp.dot(tri_ref[...], jnp.where(oh, 1.0, 0.0).astype(BF16), preferred_element_type=F32)
    cnt = cnt_scr[...]
    rank = jnp.sum(jnp.where(oh, cum + cnt, 0.0), axis=1, keepdims=True) - 1.0
    cnt = cnt + cum[t - 1:t, :]
    cnt_scr[...] = cnt
    cnt_ref[...] = cnt

    for s in range(d // LANES):
        h3_ref[pl.ds(s, t, stride=SUBLANES), :] = h[:, s * LANES:(s + 1) * LANES]
    lane8 = lax.broadcasted_iota(I32, (t, 8), 1)
    wts_ref[...] = jnp.where(lane8 == 0, wa, jnp.where(lane8 == 1, wb, 0.0))
    meta_ref[...] = jnp.where(lane8 == 0, cls, jnp.where(lane8 == 1, rank.astype(I32), 0))


def _route0_kernel(x_ref, g_ref, wr_ref, br_ref, tri_ref, tbl_ref, h3_ref, wts_ref, meta_ref, cnt_ref, cnt_scr):
    _route_body(x_ref[...], g_ref, wr_ref, br_ref, tri_ref, tbl_ref, h3_ref, wts_ref, meta_ref, cnt_ref, cnt_scr)


def _route1_kernel(x_ref, o_ref, wo_ref, g_ref, wr_ref, br_ref, tri_ref, tbl_ref,
                   xo_ref, h3_ref, wts_ref, meta_ref, cnt_ref, cnt_scr):
    x = x_ref[...] + jnp.dot(o_ref[...], wo_ref[...], preferred_element_type=F32)
    xo_ref[...] = x
    _route_body(x, g_ref, wr_ref, br_ref, tri_ref, tbl_ref, h3_ref, wts_ref, meta_ref, cnt_ref, cnt_scr)


def _route(x, g, w_rg, b_rg, w_re, b_re, attn_o=None, w_o=None):
    n, d = x.shape
    assert d == SUBLANES * LANES, "one (8, 128) tile per token row"
    t = min(TILE_ROUTE, n)
    wr = jnp.zeros((d, LANES), F32)
    wr = wr.at[:, :N_GROUPS].set(w_rg)
    wr = wr.at[:, N_GROUPS:N_GROUPS + N_GROUPS * N_EXP].set(jnp.transpose(w_re, (1, 0, 2)).reshape(d, -1))
    br = jnp.zeros((1, LANES), F32)
    br = br.at[0, :N_GROUPS].set(b_rg)
    br = br.at[0, N_GROUPS:N_GROUPS + N_GROUPS * N_EXP].set(b_re.reshape(-1))
    tri = jnp.asarray(np.tril(np.ones((t, t), np.float32)), BF16)
    tbl = jnp.asarray(_PAIR_TBL)
    full = lambda a: pl.BlockSpec(a.shape, lambda i: (0,) * a.ndim)
    rows = lambda w: pl.BlockSpec((t, w), lambda i: (i, 0))
    common = (g.reshape(1, -1), wr.astype(BF16), br, tri, tbl)
    out_shape = [jax.ShapeDtypeStruct((n * SUBLANES, LANES), F32), jax.ShapeDtypeStruct((n, 8), F32),
                 jax.ShapeDtypeStruct((n, 8), I32), jax.ShapeDtypeStruct((1, LANES), F32)]
    out_specs = [pl.BlockSpec((t * SUBLANES, LANES), lambda i: (i, 0)), rows(8), rows(8),
                 pl.BlockSpec((1, LANES), lambda i: (0, 0))]
    if attn_o is None:
        h3, wts, meta, cnt = pl.pallas_call(
            _route0_kernel, out_shape=out_shape, grid=(n // t,),
            in_specs=[rows(d)] + [full(a) for a in common], out_specs=out_specs,
            scratch_shapes=[pltpu.VMEM((1, LANES), F32)],
            compiler_params=_params("arbitrary"), name="moe_route",
        )(x, *common)
        return x, h3, wts, meta, cnt
    wo = w_o.astype(BF16)
    xo, h3, wts, meta, cnt = pl.pallas_call(
        _route1_kernel, out_shape=[jax.ShapeDtypeStruct((n, d), F32)] + out_shape, grid=(n // t,),
        in_specs=[rows(d), rows(attn_o.shape[1]), full(wo)] + [full(a) for a in common],
        out_specs=[rows(d)] + out_specs,
        scratch_shapes=[pltpu.VMEM((1, LANES), F32)],
        compiler_params=_params("arbitrary"), name="attn_out_moe_route",
    )(x, attn_o, wo, *common)
    return xo, h3, wts, meta, cnt


def _plan(meta, cnt, n):
    tm = TILE_EXP
    nt_max = n // tm + N_CLASSES
    cnt = cnt[0, :N_CLASSES].astype(I32)
    nt = (cnt + tm - 1) // tm
    tend = jnp.cumsum(nt)
    offs = (tend - nt) * tm
    total = tend[-1]
    classes = jnp.arange(N_CLASSES, dtype=I32)
    dest = jnp.sum(jnp.where(meta[:, 0:1] == classes[None, :], offs[None, :], 0), axis=1) + meta[:, 1]
    tile = jnp.arange(nt_max, dtype=I32)
    rowblk = jnp.minimum(tile, total - 1)
    tcls = jnp.sum((tend[None, :] <= rowblk[:, None]).astype(I32), axis=1)
    active = (tile < total).astype(I32)
    exp_a = jnp.take(jnp.asarray(_CLASS_EXP_A), tcls)
    exp_b = jnp.take(jnp.asarray(_CLASS_EXP_B), tcls)
    pad = jnp.zeros((LANES,), I32)
    padstart = pad.at[:N_CLASSES].set(offs + cnt)
    padstart = padstart.at[N_CLASSES].set(total * 2)
    padlen = pad.at[:N_CLASSES].set(nt * tm - cnt)
    return dest, rowblk, active, exp_a, exp_b, padstart, padlen, nt_max


_PAD_BITS = tuple(1 << b for b in reversed(range(int(math.log2(TILE_EXP)))))
DMA_UNROLL = 8


def _dispatch_kernel(dest_ref, padstart_ref, padlen_ref, h3_ref, hs_ref, zbuf, sem):
    rows = h3_ref.shape[0]
    t = rows // SUBLANES
    i = pl.program_id(0)

    def issue(g, carry):
        for u in range(DMA_UNROLL):
            r = g * DMA_UNROLL + u
            d = dest_ref[i * t + r]
            pltpu.make_async_copy(h3_ref.at[pl.ds(pl.multiple_of(r * SUBLANES, SUBLANES), SUBLANES)],
                                  hs_ref.at[pl.ds(pl.multiple_of(d * SUBLANES, SUBLANES), SUBLANES)],
                                  sem.at[0]).start(priority=u % 2)
        return carry

    lax.fori_loop(0, t // DMA_UNROLL, issue, 0)
    pltpu.make_async_copy(h3_ref, hs_ref.at[pl.ds(0, rows)], sem.at[0]).wait()

    @pl.when(i == pl.num_programs(0) - 1)
    def _():
        zbuf[...] = jnp.zeros(zbuf.shape, F32)

        def pad_copy(start, size):
            return pltpu.make_async_copy(zbuf.at[pl.ds(0, size * SUBLANES)],
                                         hs_ref.at[pl.ds(pl.multiple_of(start * SUBLANES, SUBLANES), size * SUBLANES)],
                                         sem.at[1])

        def pieces(c, fn):
            start = padstart_ref[c]
            left = padlen_ref[c]
            for bit in _PAD_BITS:
                has = (left & bit) != 0

                @pl.when(has)
                def _(start=start, bit=bit):
                    fn(pad_copy(start, bit))

                start = start + jnp.where(has, bit, 0)

        def issue_pad(c, carry):
            pieces(c, lambda cp: cp.start())
            return carry

        def drain_pad(c, carry):
            pieces(c, lambda cp: cp.wait())
            return carry

        lax.fori_loop(0, N_CLASSES, issue_pad, 0)
        lax.fori_loop(0, N_CLASSES, drain_pad, 0)

        half = zbuf.shape[0] // SUBLANES
        first_unused = padstart_ref[N_CLASSES]
        n_halves = hs_ref.shape[0] // zbuf.shape[0]

        def issue_tail(j, carry):
            pad_copy(j * half, half).start()
            return carry

        def drain_tail(j, carry):
            pad_copy(j * half, half).wait()
            return carry

        lax.fori_loop(first_unused, n_halves, issue_tail, 0)
        lax.fori_loop(first_unused, n_halves, drain_tail, 0)


def _dispatch(h3, dest, padstart, padlen, nt_max):
    n = h3.shape[0] // SUBLANES
    t = min(TILE_DISP, n)
    return pl.pallas_call(
        _dispatch_kernel,
        out_shape=jax.ShapeDtypeStruct((nt_max * TILE_EXP * SUBLANES, LANES), F32),
        grid_spec=pltpu.PrefetchScalarGridSpec(
            num_scalar_prefetch=3, grid=(n // t,),
            in_specs=[pl.BlockSpec((t * SUBLANES, LANES), lambda i, *_: (i, 0))],
            out_specs=pl.BlockSpec(memory_space=pl.ANY),
            scratch_shapes=[pltpu.VMEM((TILE_EXP // 2 * SUBLANES, LANES), F32), pltpu.SemaphoreType.DMA((2,))]),
        compiler_params=_params("arbitrary"), name="moe_dispatch",
    )(dest, padstart, padlen, h3)


def _expert_kernel(rowblk_ref, active_ref, ea_ref, eb_ref, hs_ref,
                   wga_ref, wua_ref, wda_ref, wgb_ref, wub_ref, wdb_ref, ys_ref, rec_scr):
    tm = hs_ref.shape[0] // SUBLANES
    i = pl.program_id(0)

    @pl.when(active_ref[i] == 1)
    def _():
        x = jnp.concatenate([hs_ref[pl.ds(s, tm, stride=SUBLANES), :] for s in range(SUBLANES)], axis=1).astype(BF16)
        for slot, (wg, wu, wd) in enumerate(((wga_ref, wua_ref, wda_ref), (wgb_ref, wub_ref, wdb_ref))):
            hg = jnp.dot(x, wg[0], preferred_element_type=F32)
            ug = jnp.dot(x, wu[0], preferred_element_type=F32)
            act = hg * jax.nn.sigmoid(hg) * ug
            y = jnp.dot(act.astype(BF16), wd[0], preferred_element_type=F32)
            for s in range(SUBLANES):
                rec_scr[slot, pl.ds(s, tm, stride=SUBLANES), :] = y[:, s * LANES:(s + 1) * LANES]
        rec = jnp.stack([rec_scr[0].reshape(tm, SUBLANES, LANES), rec_scr[1].reshape(tm, SUBLANES, LANES)], axis=1)
        ys_ref[...] = rec.reshape(tm * 2 * SUBLANES, LANES).astype(ys_ref.dtype)

    @pl.when(active_ref[i] == 0)
    def _():
        ys_ref[...] = jnp.zeros(ys_ref.shape, ys_ref.dtype)


def _experts(hs, rowblk, active, exp_a, exp_b, w_gate, w_up, w_down):
    tm = TILE_EXP
    rows = hs.shape[0] // SUBLANES
    d = SUBLANES * LANES
    f = w_gate.shape[-1]
    wg = w_gate.reshape(-1, d, f).astype(BF16)
    wu = w_up.reshape(-1, d, f).astype(BF16)
    wd = w_down.reshape(-1, f, d).astype(BF16)
    in_a = lambda shp: pl.BlockSpec((1,) + shp, lambda i, rb, ac, ea, eb: (ea[i], 0, 0))
    in_b = lambda shp: pl.BlockSpec((1,) + shp, lambda i, rb, ac, ea, eb: (eb[i], 0, 0))
    return pl.pallas_call(
        _expert_kernel,
        out_shape=jax.ShapeDtypeStruct((rows * 2 * SUBLANES, LANES), BF16),
        grid_spec=pltpu.PrefetchScalarGridSpec(
            num_scalar_prefetch=4, grid=(rows // tm,),
            in_specs=[pl.BlockSpec((tm * SUBLANES, LANES), lambda i, rb, ac, ea, eb: (rb[i], 0)),
                      in_a((d, f)), in_a((d, f)), in_a((f, d)), in_b((d, f)), in_b((d, f)), in_b((f, d))],
            out_specs=pl.BlockSpec((tm * 2 * SUBLANES, LANES), lambda i, *_: (i, 0)),
            scratch_shapes=[pltpu.VMEM((2, tm * SUBLANES, LANES), F32)]),
        compiler_params=_params("arbitrary"), name="moe_experts",
    )(rowblk, active, exp_a, exp_b, hs, wg, wu, wd, wg, wu, wd)


def _ple_kernel(dest_ref, x_ref, wts_ref, p_ref, g_ref, wg_ref, wp_ref, fg_ref, ys_ref, o_ref, ybuf0, ybuf1, rec_scr,
                sem, *, final):
    t = x_ref.shape[0]
    rec = 2 * SUBLANES
    i = pl.program_id(0)
    last = pl.num_programs(0) - 1

    def record_copy(d, row, buf, s):
        return pltpu.make_async_copy(ys_ref.at[pl.ds(pl.multiple_of(d * rec, rec), rec)],
                                     buf.at[pl.ds(row, rec)], sem.at[s])

    def tile_wait(buf, s):
        pltpu.make_async_copy(ys_ref.at[pl.ds(0, t * rec)], buf, sem.at[s]).wait()

    @pl.when(i == 0)
    def _():
        def issue(r, carry):
            record_copy(dest_ref[r], pl.multiple_of(r * rec, rec), ybuf0, 0).start()
            return carry
        lax.fori_loop(0, t, issue, 0, unroll=DMA_UNROLL)

    def step(cur, cs, nxt, ns):
        tile_wait(cur, cs)
        base = jnp.minimum(i + 1, last) * t
        for r in range(t):
            record_copy(dest_ref[base + r], r * rec, nxt, ns).start(priority=r % 2)

        recs = cur[...].astype(F32).reshape(t, 2, SUBLANES, LANES)
        wts = wts_ref[...]
        x = x_ref[...]
        for e in range(2):
            rec_scr[e] = recs[:, e].reshape(t * SUBLANES, LANES)
            ye = jnp.concatenate([rec_scr[e, pl.ds(s, t, stride=SUBLANES), :] for s in range(SUBLANES)], axis=1)
            x = x + wts[:, e:e + 1] * ye
        gate = jax.nn.sigmoid(jnp.dot(_rms(x, g_ref[...]).astype(BF16), wg_ref[...], preferred_element_type=F32))
        x = x + gate * jnp.dot(p_ref[0].astype(BF16), wp_ref[...], preferred_element_type=F32)
        if final:
            x = _rms(x, fg_ref[...])
        o_ref[...] = x

        @pl.when(i == last)
        def _():
            tile_wait(nxt, ns)

    @pl.when(i % 2 == 0)
    def _():
        step(ybuf0, 0, ybuf1, 1)

    @pl.when(i % 2 == 1)
    def _():
        step(ybuf1, 1, ybuf0, 0)


def _combine_ple(x, ys, dest, wts, p, layer, g, w_gate, w_proj, final_g, final):
    n, d = x.shape
    t = min(TILE_PLE, n)
    rec = 2 * SUBLANES
    full = lambda a: pl.BlockSpec(a.shape, lambda i, *_: (0,) * a.ndim)
    rows = lambda w: pl.BlockSpec((t, w), lambda i, *_: (i, 0))
    args = (g.reshape(1, -1), w_gate.astype(BF16), w_proj.astype(BF16), final_g.reshape(1, -1))
    return pl.pallas_call(
        functools.partial(_ple_kernel, final=final),
        out_shape=jax.ShapeDtypeStruct((n, d), F32),
        grid_spec=pltpu.PrefetchScalarGridSpec(
            num_scalar_prefetch=1, grid=(n // t,),
            in_specs=[rows(d), rows(wts.shape[1]), pl.BlockSpec((1, t, p.shape[2]), lambda i, *_: (layer, i, 0))]
                     + [full(a) for a in args] + [pl.BlockSpec(memory_space=pl.ANY)],
            out_specs=rows(d),
            scratch_shapes=[pltpu.VMEM((t * rec, LANES), ys.dtype), pltpu.VMEM((t * rec, LANES), ys.dtype),
                            pltpu.VMEM((2, t * SUBLANES, LANES), F32), pltpu.SemaphoreType.DMA((2,))]),
        compiler_params=_params("arbitrary"), name="moe_combine_ple",
    )(dest, x, wts, p, *args, ys)


def _moe_ple(x, p, layer, g_ffn, w_rg, b_rg, w_re, b_re, w_gate, w_up, w_down, ple_g, ple_wg, ple_wp, final_g, final,
             attn_o=None, w_o=None):
    n = x.shape[0]
    x, h3, wts, meta, cnt = _route(x, g_ffn, w_rg, b_rg, w_re, b_re, attn_o, w_o)
    dest, rowblk, active, exp_a, exp_b, padstart, padlen, nt_max = _plan(meta, cnt, n)
    hs = _dispatch(h3, dest, padstart, padlen, nt_max)
    ys = _experts(hs, rowblk, active, exp_a, exp_b, w_gate, w_up, w_down)
    return _combine_ple(x, ys, dest, wts, p, layer, ple_g, ple_wg, ple_wp, final_g, final)


def _qkv_kernel(x_ref, pos_ref, g_ref, wq_ref, wk_ref, wv_ref, invf_ref, q_ref, k_ref, vt_ref, *, qscale):
    t = x_ref.shape[1]
    h = _rms(x_ref[0], g_ref[...]).astype(BF16)
    lane = lax.broadcasted_iota(I32, (t, LANES), 1) % HEAD_DIM
    ang = pos_ref[0].astype(F32) * invf_ref[...]
    cos = jnp.cos(ang)
    sin = jnp.sin(ang)
    half = ROPE_DIM // 2
    cmul = jnp.where(lane < ROPE_DIM, cos, 1.0)
    s_up = jnp.where(lane < half, -sin, 0.0)
    s_dn = jnp.where(lane >= half, jnp.where(lane < ROPE_DIM, sin, 0.0), 0.0)

    def rope(w_ref, o_ref, scale):
        y = jnp.dot(h, w_ref[...], preferred_element_type=F32)
        for c0 in range(0, y.shape[1], LANES):
            yb = y[:, c0:c0 + LANES]
            rot = yb * cmul + pltpu.roll(yb, LANES - half, 1) * s_up + pltpu.roll(yb, half, 1) * s_dn
            o_ref[0, :, c0:c0 + LANES] = (rot * scale).astype(o_ref.dtype)

    rope(wq_ref, q_ref, qscale)
    rope(wk_ref, k_ref, 1.0)
    v = jnp.dot(h, wv_ref[...], preferred_element_type=F32)
    vt_ref[0] = v.T.astype(vt_ref.dtype)


def _qkv(x, positions, g, w_qkv):
    b, s, d = x.shape
    t = min(TILE_QKV, s)
    nq = (w_qkv.shape[1] - d) // 2
    w = w_qkv.astype(BF16)
    wq, wk, wv = w[:, :nq], w[:, nq:2 * nq], w[:, 2 * nq:]
    inv_freq = ROPE_THETA ** (-jnp.arange(0, ROPE_DIM, 2, dtype=F32) / ROPE_DIM)
    lane = np.arange(LANES) % HEAD_DIM
    invf = jnp.where(lane < ROPE_DIM, jnp.take(inv_freq, lane % (ROPE_DIM // 2)), 0.0).reshape(1, LANES)
    qscale = HEAD_DIM ** -0.5 * math.log2(math.e)
    full = lambda a: pl.BlockSpec(a.shape, lambda i, j: (0,) * a.ndim)
    args = (g.reshape(1, -1), wq, wk, wv, invf)
    return pl.pallas_call(
        functools.partial(_qkv_kernel, qscale=qscale),
        out_shape=[jax.ShapeDtypeStruct((b, s, nq), BF16), jax.ShapeDtypeStruct((b, s, nq), BF16),
                   jax.ShapeDtypeStruct((b, d, s), BF16)],
        grid=(b, s // t),
        in_specs=[pl.BlockSpec((1, t, d), lambda i, j: (i, j, 0)), pl.BlockSpec((1, t, 1), lambda i, j: (i, j, 0))]
                 + [full(a) for a in args],
        out_specs=[pl.BlockSpec((1, t, nq), lambda i, j: (i, j, 0)), pl.BlockSpec((1, t, nq), lambda i, j: (i, j, 0)),
                   pl.BlockSpec((1, d, t), lambda i, j: (i, 0, j))],
        compiler_params=_params("arbitrary", "arbitrary"), name="qkv_rope",
    )(x, positions.reshape(b, s, 1), *args)


def _attn_kernel(q_ref, k_ref, vt_ref, lam_ref, sg_ref, o_ref, m_scr, l_scr, acc_scr, st_a, st_b, *, lam_init):
    tq = q_ref.shape[1]
    tk = tq
    qi = pl.program_id(2)
    q = q_ref[0]
    qs = [q[:, c * HEAD_DIM:(c + 1) * HEAD_DIM] for c in range(2)]
    m_scr[...] = jnp.full(m_scr.shape, -1e30, F32)
    l_scr[...] = jnp.zeros(l_scr.shape, F32)
    acc_scr[...] = jnp.zeros(acc_scr.shape, F32)

    def scores(buf, j):
        kb = k_ref[0, pl.ds(pl.multiple_of(j * tk, tk), tk), :]
        for c in range(2):
            buf[c] = lax.dot_general(kb[:, c * HEAD_DIM:(c + 1) * HEAD_DIM], qs[c], (((1,), (1,)), ((), ())),
                                     preferred_element_type=F32)

    def accumulate(buf, j, masked):
        parts = ((0, tk // 2, 0), (tk // 2, tk // 2, tq // 2)) if masked else ((0, tk, 0),)
        for k_lo, k_len, q_lo in parts:
            vtb = vt_ref[0, :, pl.ds(pl.multiple_of(j * tk + k_lo, k_len), k_len)]
            if masked:
                keep = (lax.broadcasted_iota(I32, (k_len, tq - q_lo), 1) + q_lo
                        >= lax.broadcasted_iota(I32, (k_len, tq - q_lo), 0) + k_lo)
            for c in range(2):
                st = buf[c, k_lo:k_lo + k_len, q_lo:]
                if masked:
                    st = jnp.where(keep, st, -1e30)
                m_old = m_scr[c:c + 1, q_lo:]
                m_new = jnp.maximum(m_old, jnp.max(st, axis=0, keepdims=True))
                alpha = jnp.exp2(m_old - m_new)
                p = jnp.exp2(st - m_new)
                l_scr[c:c + 1, q_lo:] = alpha * l_scr[c:c + 1, q_lo:] + jnp.sum(p, axis=0, keepdims=True)
                m_scr[c:c + 1, q_lo:] = m_new
                acc_scr[c, :, q_lo:] = (alpha * acc_scr[c, :, q_lo:]
                                        + jnp.dot(vtb, p.astype(BF16), preferred_element_type=F32))

    scores(st_a, 0)

    def pair(jp, carry):
        j = 2 * jp
        scores(st_b, j + 1)
        accumulate(st_a, j, False)
        scores(st_a, j + 2)
        accumulate(st_b, j + 1, False)
        return carry

    lax.fori_loop(0, qi // 2, pair, 0)

    @pl.when(qi % 2 == 1)
    def _():
        scores(st_b, qi)
        accumulate(st_a, qi - 1, False)
        accumulate(st_b, qi, True)

    @pl.when(qi % 2 == 0)
    def _():
        accumulate(st_a, qi, True)

    lp = lam_ref[...]
    lam = (jnp.exp(jnp.sum(lp[0:1] * lp[1:2], axis=1, keepdims=True))
           - jnp.exp(jnp.sum(lp[2:3] * lp[3:4], axis=1, keepdims=True)) + lam_init)
    ot = acc_scr[0] / l_scr[0:1, :] - lam * (acc_scr[1] / l_scr[1:2, :])
    ot = ot * lax.rsqrt(jnp.mean(ot * ot, axis=0, keepdims=True) + EPS) * sg_ref[...] * (1.0 - lam_init)
    o_ref[0] = ot.T.astype(o_ref.dtype)


def _attention(q, k, vt, lam_params, subln_g, lam_init):
    b, s, nq = q.shape
    heads = nq // (2 * HEAD_DIM)
    tq = min(TILE_Q, s)
    return pl.pallas_call(
        functools.partial(_attn_kernel, lam_init=lam_init),
        out_shape=jax.ShapeDtypeStruct((b, s, heads * V_DIM), BF16),
        grid=(b, heads, s // tq),
        in_specs=[pl.BlockSpec((1, tq, 2 * HEAD_DIM), lambda i, h, j: (i, j, h)),
                  pl.BlockSpec((1, s, 2 * HEAD_DIM), lambda i, h, j: (i, 0, h)),
                  pl.BlockSpec((1, V_DIM, s), lambda i, h, j: (i, h, 0)),
                  pl.BlockSpec(lam_params.shape, lambda i, h, j: (0, 0)),
                  pl.BlockSpec((V_DIM, 1), lambda i, h, j: (0, 0))],
        out_specs=pl.BlockSpec((1, tq, V_DIM), lambda i, h, j: (i, j, h)),
        scratch_shapes=[pltpu.VMEM((2, tq), F32), pltpu.VMEM((2, tq), F32), pltpu.VMEM((2, V_DIM, tq), F32),
                        pltpu.VMEM((2, tq, tq), F32), pltpu.VMEM((2, tq, tq), F32)],
        compiler_params=_params("arbitrary", "arbitrary", "arbitrary"), name="diff_attention",
    )(q, k, vt, lam_params, subln_g.reshape(V_DIM, 1))


def kernel(x, p, positions, norm_mix, norm_ffn, conv_w_pw1, conv_b_pw1, conv_w_dw, conv_b_dw, conv_ln_g, conv_ln_b,
           conv_w_pw2, conv_b_pw2, da_w_qkv, da_lambda, da_subln, da_w_o, moe_w_rg, moe_b_rg, moe_w_re, moe_b_re,
           moe_w_gate, moe_w_up, moe_w_down, ple_norm, ple_w_gate, ple_w_proj, final_norm):
    b, s, d = x.shape
    n = b * s
    depth = norm_mix.shape[0]
    assert depth == 2, "layer 0 is the conv mixer, layer 1 differential attention"
    pf = p.reshape(depth, n, -1)

    def moe(i, xin, final, attn_o=None, w_o=None):
        return _moe_ple(xin, pf, i, norm_ffn[i], moe_w_rg[i], moe_b_rg[i], moe_w_re[i], moe_b_re[i],
                        moe_w_gate[i], moe_w_up[i], moe_w_down[i], ple_norm[i], ple_w_gate[i], ple_w_proj[i],
                        final_norm, final, attn_o, w_o)

    x = _conv_mixer(x, norm_mix[0], conv_w_pw1[0], conv_b_pw1[0], conv_w_dw[0], conv_b_dw[0],
                    conv_ln_g[0], conv_ln_b[0], conv_w_pw2[0], conv_b_pw2[0])
    x = moe(0, x.reshape(n, d), False)

    lam_init = 0.8 - 0.6 * math.exp(-0.3 * 1)
    q, k, vt = _qkv(x.reshape(b, s, d), positions, norm_mix[1], da_w_qkv[0])
    o = _attention(q, k, vt, da_lambda[0], da_subln[0], lam_init)
    x = moe(1, x, True, o.reshape(n, -1), da_w_o[0])
    return x.reshape(b, s, d)
```
